```python
import jax, jax.numpy as jnp
from jax import lax
import numpy as np

D_MODEL = 2048
BATCH = 4
SEQ = 8192
DEPTH = 1
DEC_BATCH = 8
DEC_SEQ = 32
PAST_LEN = 2048

CHUNK = 64
GDN_HEADS = 8
GDN_DK = 128
GDN_DV = 128
GDN_KEY_DIM = GDN_HEADS * GDN_DK
GDN_VAL_DIM = GDN_HEADS * GDN_DV
GDN_QKV_DIM = 2 * GDN_KEY_DIM + GDN_VAL_DIM
CONV_W = 4
GDN_CHUNK = 64
MLA_HEADS = 8
Q_LORA = 512
KV_LORA = 512
NOPE_DIM = 128
ROPE_DIM = 64
V_DIM = 128
MLA_WIDTH = MLA_HEADS * V_DIM
ROPE_THETA = 10000.0
Q_BLOCK = 128
MIX_WIDTH = GDN_VAL_DIM + MLA_WIDTH
IN_SIZES = (GDN_QKV_DIM, GDN_VAL_DIM, GDN_HEADS, GDN_HEADS, Q_LORA, KV_LORA, ROPE_DIM, MLA_WIDTH)
IN_DIM = GDN_QKV_DIM + GDN_VAL_DIM + 2 * GDN_HEADS + Q_LORA + KV_LORA + ROPE_DIM + MLA_WIDTH
EPS = 1e-6

kernel_name = 'hymba_gdn_mla_streaming_step'


def rmsnorm(x, w):
    xf = x.astype(jnp.float32)
    y = xf * lax.rsqrt(jnp.mean(xf * xf, axis=-1, keepdims=True) + EPS)
    return (y * w.astype(jnp.float32)).astype(x.dtype)


def l2norm(x):
    xf = x.astype(jnp.float32)
    return xf * lax.rsqrt(jnp.sum(xf * xf, axis=-1, keepdims=True) + EPS)


def rope_tables(pos):
    inv = ROPE_THETA ** (-jnp.arange(0, ROPE_DIM, 2, dtype=jnp.float32) / ROPE_DIM)
    ang = pos.astype(jnp.float32)[:, None] * inv[None, :]
    return jnp.cos(ang), jnp.sin(ang)


def apply_rope(x, cos, sin):
    xf = x.astype(jnp.float32)
    extra = x.ndim - 3
    c = cos.reshape((cos.shape[0],) + (1,) * extra + (cos.shape[1],))
    s = sin.reshape((sin.shape[0],) + (1,) * extra + (sin.shape[1],))
    half = ROPE_DIM // 2
    x1, x2 = xf[..., :half], xf[..., half:]
    return jnp.concatenate([x1 * c - x2 * s, x2 * c + x1 * s], axis=-1).astype(x.dtype)


def causal_short_conv(x, buf, w):
    t = x.shape[1]
    xp = jnp.concatenate([buf.astype(x.dtype), x], axis=1)
    y = xp[:, 0:t] * w[0]
    for i in range(1, CONV_W):
        y = y + xp[:, i:i + t] * w[i]
    return jax.nn.silu(y), xp[:, -(CONV_W - 1):]


def gated_delta_rule(q, k, v, g, beta, s0):
    b, h, t, _ = q.shape
    c = GDN_CHUNK
    pad = (-t) % c
    n = (t + pad) // c

    def blocks(a):
        a = jnp.pad(a, [(0, 0), (0, 0), (0, pad)] + [(0, 0)] * (a.ndim - 3))
        return a.reshape(b, h, n, c, *a.shape[3:])

    q = blocks(q * (GDN_DK ** -0.5))
    k = blocks(k)
    v = blocks(v)
    beta = blocks(beta)
    g = jnp.cumsum(blocks(g), axis=-1)
    incl = jnp.tril(jnp.ones((c, c), dtype=bool))
    strict = jnp.tril(jnp.ones((c, c), dtype=bool), -1)
    decay = jnp.exp(jnp.where(incl, g[..., :, None] - g[..., None, :], -jnp.inf))
    k_beta = k * beta[..., None]
    lmat = jnp.where(strict, jnp.einsum('bhnid,bhnjd->bhnij', k_beta, k) * decay, 0.0)
    eye = jnp.eye(c, dtype=q.dtype)
    tmat = lax.linalg.triangular_solve(lmat + eye, jnp.broadcast_to(eye, lmat.shape),
                                       left_side=True, lower=True)
    u = jnp.einsum('bhnij,bhnjv->bhniv', tmat, v * beta[..., None])
    w = jnp.einsum('bhnij,bhnjd->bhnid', tmat, k_beta * jnp.exp(g)[..., None])
    attn = jnp.einsum('bhnid,bhnjd->bhnij', q, k) * decay

    def step(s, xs):
        q_c, k_c, u_c, w_c, g_c, a_c = xs
        v_new = u_c - jnp.einsum('bhid,bhdv->bhiv', w_c, s)
        o_c = (jnp.einsum('bhid,bhdv->bhiv', q_c * jnp.exp(g_c)[..., None], s)
               + jnp.einsum('bhij,bhjv->bhiv', a_c, v_new))
        g_last = g_c[..., -1]
        k_dec = k_c * jnp.exp(g_last[..., None] - g_c)[..., None]
        s = s * jnp.exp(g_last)[..., None, None] + jnp.einsum('bhid,bhiv->bhdv', k_dec, v_new)
        return s, o_c

    xs = tuple(jnp.moveaxis(a, 2, 0) for a in (q, k, u, w, g, attn))
    s, o = lax.scan(step, s0, xs)
    o = jnp.moveaxis(o, 0, 2).reshape(b, h, n * c, GDN_DV)[:, :, :t]
    return o, s


def mla_attend(q_lat, q_rope, q_pos, c_kv, k_rope, k_pos):
    b, t, h, r = q_lat.shape
    qb = min(Q_BLOCK, t)
    nb = t // qb
    scale = (NOPE_DIM + ROPE_DIM) ** -0.5

    def block(args):
        ql, qr, qp = args
        s = (jnp.einsum('bqhr,bkr->bhqk', ql, c_kv)
             + jnp.einsum('bqhd,bkd->bhqk', qr, k_rope)).astype(jnp.float32) * scale
        visible = k_pos[None, :] < ((qp // CHUNK) + 1)[:, None] * CHUNK
        s = jnp.where(visible[None, None], s, -jnp.inf)
        p = jax.nn.softmax(s, axis=-1).astype(c_kv.dtype)
        return jnp.einsum('bhqk,bkr->bqhr', p, c_kv)

    ql = jnp.moveaxis(q_lat.reshape(b, nb, qb, h, r), 1, 0)
    qr = jnp.moveaxis(q_rope.reshape(b, nb, qb, h, ROPE_DIM), 1, 0)
    qp = q_pos.reshape(nb, qb)
    out = lax.map(block, (ql, qr, qp))
    return jnp.moveaxis(out, 0, 1).reshape(b, t, h, r)


def mixer_layer(x, conv_buf, ssm0, past_lat, past_krope,
                ln_in_w, w_in, w_conv, a_log, dt_bias, gdn_norm_w,
                q_norm_w, w_q_up, kv_norm_w, w_uk, w_uv, w_out):
    b, t, _ = x.shape
    hproj = rmsnorm(x, ln_in_w) @ w_in
    points, acc = [], 0
    for size in IN_SIZES[:-1]:
        acc += size
        points.append(acc)
    qkv, z_g, b_g, a_g, cq, ckv, kr, z_m = jnp.split(hproj, points, axis=-1)

    qkv, new_conv = causal_short_conv(qkv, conv_buf, w_conv)
    q, k, v = jnp.split(qkv, [GDN_KEY_DIM, 2 * GDN_KEY_DIM], axis=-1)
    q = l2norm(q.reshape(b, t, GDN_HEADS, GDN_DK))
    k = l2norm(k.reshape(b, t, GDN_HEADS, GDN_DK))
    v = v.reshape(b, t, GDN_HEADS, GDN_DV).astype(jnp.float32)
    beta = jax.nn.sigmoid(b_g.astype(jnp.float32))
    g = -jnp.exp(a_log.astype(jnp.float32)) * jax.nn.softplus(
        a_g.astype(jnp.float32) + dt_bias.astype(jnp.float32))
    o_g, new_ssm = gated_delta_rule(q.transpose(0, 2, 1, 3), k.transpose(0, 2, 1, 3),
                                    v.transpose(0, 2, 1, 3), g.transpose(0, 2, 1),
                                    beta.transpose(0, 2, 1), ssm0.astype(jnp.float32))
    o_g = rmsnorm(o_g.transpose(0, 2, 1, 3), gdn_norm_w).reshape(b, t, GDN_VAL_DIM).astype(x.dtype)
    o_g = o_g * jax.nn.silu(z_g)

    p_len = past_lat.shape[1]
    q_pos = jnp.arange(p_len, p_len + t, dtype=jnp.int32)
    k_pos = jnp.arange(p_len + t, dtype=jnp.int32)
    cos, sin = rope_tables(q_pos)
    qm = jnp.einsum('btr,rhd->bthd', rmsnorm(cq, q_norm_w), w_q_up)
    q_nope, q_rope = qm[..., :NOPE_DIM], apply_rope(qm[..., NOPE_DIM:], cos, sin)
    c_kv = rmsnorm(ckv, kv_norm_w)
    k_rope = apply_rope(kr, cos, sin)
    q_lat = jnp.einsum('bthd,rhd->bthr', q_nope, w_uk)
    o_lat = mla_attend(q_lat, q_rope, q_pos,
                       jnp.concatenate([past_lat.astype(c_kv.dtype), c_kv], axis=1),
                       jnp.concatenate([past_krope.astype(k_rope.dtype), k_rope], axis=1), k_pos)
    o_m = jnp.einsum('bthr,rhv->bthv', o_lat, w_uv).reshape(b, t, MLA_WIDTH) * jax.nn.silu(z_m)

    y = x + jnp.concatenate([o_g, o_m], axis=-1) @ w_out
    return y, new_conv, new_ssm.astype(ssm0.dtype), c_kv, k_rope


def setup_inputs(seed: int = 0) -> dict:
    key = jax.random.key(seed)
    ks = jax.random.split(key, 24)
    f32 = jnp.float32
    nrm = lambda k, shape, s: jax.random.normal(k, shape, f32) * s
    gain = lambda k, shape: 1.0 + 0.01 * jax.random.normal(k, shape, f32)
    dt = jnp.exp(jax.random.uniform(ks[6], (DEPTH, GDN_HEADS), f32, np.log(1e-3), np.log(1e-1)))
    return {
        'x_prompt': nrm(ks[0], (BATCH, SEQ, D_MODEL), 1.0),
        'x_sample': nrm(ks[1], (DEC_BATCH, DEC_SEQ, D_MODEL), 1.0),
        'cache_mla_latent': nrm(ks[2], (DEPTH, DEC_BATCH, PAST_LEN, KV_LORA), 1.0),
        'cache_mla_krope': nrm(ks[3], (DEPTH, DEC_BATCH, PAST_LEN, ROPE_DIM), 1.0),
        'state_gdn_conv': nrm(ks[4], (DEPTH, DEC_BATCH, CONV_W - 1, GDN_QKV_DIM), 1.0),
        'state_gdn_ssm': nrm(ks[5], (DEPTH, DEC_BATCH, GDN_HEADS, GDN_DK, GDN_DV), 0.1),
        'ln_in_w': gain(ks[7], (DEPTH, D_MODEL)),
        'w_in': nrm(ks[8], (DEPTH, D_MODEL, IN_DIM), D_MODEL ** -0.5),
        'w_conv': nrm(ks[9], (DEPTH, CONV_W, GDN_QKV_DIM), CONV_W ** -0.5),
        'a_log': jnp.log(jax.random.uniform(ks[10], (DEPTH, GDN_HEADS), f32, 1.0, 16.0)),
        'dt_bias': dt + jnp.log(-jnp.expm1(-dt)),
        'gdn_norm_w': gain(ks[11], (DEPTH, GDN_DV)),
        'q_norm_w': gain(ks[12], (DEPTH, Q_LORA)),
        'w_q_up': nrm(ks[13], (DEPTH, Q_LORA, MLA_HEADS, NOPE_DIM + ROPE_DIM), Q_LORA ** -0.5),
        'kv_norm_w': gain(ks[14], (DEPTH, KV_LORA)),
        'w_uk': nrm(ks[15], (DEPTH, KV_LORA, MLA_HEADS, NOPE_DIM), KV_LORA ** -0.5),
        'w_uv': nrm(ks[16], (DEPTH, KV_LORA, MLA_HEADS, V_DIM), KV_LORA ** -0.5),
        'w_out': nrm(ks[17], (DEPTH, MIX_WIDTH, D_MODEL), MIX_WIDTH ** -0.5),
        'ln_final_w': gain(ks[18], (D_MODEL,)),
    }


def reference(x_prompt, x_sample, cache_mla_latent, cache_mla_krope, state_gdn_conv, state_gdn_ssm,
              ln_in_w, w_in, w_conv, a_log, dt_bias, gdn_norm_w, q_norm_w, w_q_up, kv_norm_w,
              w_uk, w_uv, w_out, ln_final_w):
    hp, hs = x_prompt, x_sample
    bp = x_prompt.shape[0]
    conv_p, ssm_p, lat_p, kr_p = [], [], [], []
    conv_s, ssm_s, lat_s, kr_s = [], [], [], []
    for l in range(DEPTH):
        lw = (ln_in_w[l], w_in[l], w_conv[l], a_log[l], dt_bias[l], gdn_norm_w[l],
              q_norm_w[l], w_q_up[l], kv_norm_w[l], w_uk[l], w_uv[l], w_out[l])
        hp, c1, s1, l1, r1 = mixer_layer(
            hp, jnp.zeros((bp, CONV_W - 1, GDN_QKV_DIM), hp.dtype),
            jnp.zeros((bp, GDN_HEADS, GDN_DK, GDN_DV), hp.dtype),
            jnp.zeros((bp, 0, KV_LORA), hp.dtype), jnp.zeros((bp, 0, ROPE_DIM), hp.dtype), *lw)
        hs, c2, s2, l2, r2 = mixer_layer(
            hs, state_gdn_conv[l], state_gdn_ssm[l], cache_mla_latent[l], cache_mla_krope[l], *lw)
        conv_p.append(c1); ssm_p.append(s1); lat_p.append(l1); kr_p.append(r1)
        conv_s.append(c2); ssm_s.append(s2); lat_s.append(l2); kr_s.append(r2)
    y_prompt = rmsnorm(hp, ln_final_w)
    y_sample = rmsnorm(hs, ln_final_w)
    return (y_prompt, y_sample,
            jnp.stack(conv_p), jnp.stack(ssm_p), jnp.stack(lat_p), jnp.stack(kr_p),
            jnp.stack(conv_s), jnp.stack(ssm_s), jnp.stack(lat_s), jnp.stack(kr_s))
```

```python
import functools
import math

import jax
import jax.numpy as jnp
from jax import lax
from jax.experimental import pallas as pl
from jax.experimental.pallas import tpu as pltpu

F32 = jnp.float32
BF16 = jnp.bfloat16

EPS = 1e-6
CHUNK = 64
GDN_CHUNK = 64
GDN_HEADS = 8
GDN_DK = 128
GDN_DV = 128
CONV_W = 4
MLA_HEADS = 8
Q_LORA = 512
KV_LORA = 512
NOPE_DIM = 128
ROPE_DIM = 64
V_DIM = 128
ROPE_THETA = 10000.0
QK_DIM = NOPE_DIM + ROPE_DIM

LANE = 128
VMEM_LIMIT = 56 * 1024 * 1024

_QB, _KB, _VB, _ZGB, _ZMB = 0, 8, 16, 24, 32
_CQ_BLK, _CKV_BLK = 10, 11
_MAIN_COLS = 6144
_SMALL_COLS = 256


def _nt(a, b):
    return lax.dot_general(a, b, (((1,), (1,)), ((), ())), preferred_element_type=F32)


def _tn(a, b):
    return lax.dot_general(a, b, (((0,), (0,)), ((), ())), preferred_element_type=F32)


def _dot(a, b):
    return jnp.dot(a, b, preferred_element_type=F32)


def _rms(x, w):
    return x * lax.rsqrt(jnp.mean(x * x, axis=-1, keepdims=True) + EPS) * w


def _sigmoid(x):
    return 1.0 / (1.0 + jnp.exp(-x))


def _params(*sem):
    return pltpu.CompilerParams(dimension_semantics=sem, vmem_limit_bytes=VMEM_LIMIT)


def _in_proj_kernel(x_ref, lnw_ref, wm_ref, ws_ref, main_ref, small_ref, h_ref, *, rows):
    @pl.when(pl.program_id(1) == 0)
    def _():
        def body(r, c):
            sl = pl.ds(pl.multiple_of(r * rows, rows), rows)
            h_ref[sl, :] = _rms(x_ref[sl, :], lnw_ref[...]).astype(BF16)
            return c
        lax.fori_loop(0, x_ref.shape[0] // rows, body, 0)
        small_ref[...] = _dot(h_ref[...], ws_ref[...])

    main_ref[...] = _dot(h_ref[...], wm_ref[...]).astype(BF16)


def _in_proj(x2d, ln_w, w_main, w_small, tm, tn):
    n, d = x2d.shape
    nm, ns = w_main.shape[1], w_small.shape[1]
    rows = min(tm, 128)
    return pl.pallas_call(
        functools.partial(_in_proj_kernel, rows=rows),
        grid=(n // tm, nm // tn),
        in_specs=[
            pl.BlockSpec((tm, d), lambda i, j: (i, 0)),
            pl.BlockSpec((1, d), lambda i, j: (0, 0)),
            pl.BlockSpec((d, tn), lambda i, j: (0, j)),
            pl.BlockSpec((d, ns), lambda i, j: (0, 0)),
        ],
        out_specs=[
            pl.BlockSpec((tm, tn), lambda i, j: (i, j)),
            pl.BlockSpec((tm, ns), lambda i, j: (i, 0)),
        ],
        out_shape=[
            jax.ShapeDtypeStruct((n, nm), BF16),
            jax.ShapeDtypeStruct((n, ns), F32),
        ],
        scratch_shapes=[pltpu.VMEM((tm, d), BF16)],
        compiler_params=_params("parallel", "arbitrary"),
        name="in_proj",
    )(x2d, ln_w, w_main, w_small)


def _out_proj_kernel(x_ref, og_ref, om_ref, wg_ref, wm_ref, lnf_ref, y_ref):
    acc = _dot(og_ref[...], wg_ref[...]) + _dot(om_ref[...], wm_ref[...])
    y_ref[...] = _rms(x_ref[...] + acc, lnf_ref[...])


def _out_proj(x2d, o_g, o_m, w_g, w_m, lnf_w, tm):
    n, d = x2d.shape
    kg, km = o_g.shape[1], o_m.shape[1]
    return pl.pallas_call(
        _out_proj_kernel,
        grid=(n // tm,),
        in_specs=[
            pl.BlockSpec((tm, d), lambda i: (i, 0)),
            pl.BlockSpec((tm, kg), lambda i: (i, 0)),
            pl.BlockSpec((tm, km), lambda i: (i, 0)),
            pl.BlockSpec((kg, d), lambda i: (0, 0)),
            pl.BlockSpec((km, d), lambda i: (0, 0)),
            pl.BlockSpec((1, d), lambda i: (0, 0)),
        ],
        out_specs=pl.BlockSpec((tm, d), lambda i: (i, 0)),
        out_shape=jax.ShapeDtypeStruct((n, d), F32),
        compiler_params=_params("parallel"),
        name="out_proj",
    )(x2d, o_g, o_m, w_g, w_m, lnf_w)


def _gdn_kernel(q_ref, k_ref, v_ref, z_ref, sm_ref, cq0_ref, ck0_ref, cv0_ref, s0_ref,
                wq_ref, wk_ref, wv_ref, gp_ref, gn_ref, o_ref, sout_ref, xbuf, s_ref, *, tb, c):
    h = pl.program_id(1)
    t = pl.program_id(2)
    dk = GDN_DK

    @pl.when(t == 0)
    def _():
        xbuf[0:8, :] = jnp.zeros((8, 3 * dk), F32)
        xbuf[8 - (CONV_W - 1):8, 0:dk] = cq0_ref[0]
        xbuf[8 - (CONV_W - 1):8, dk:2 * dk] = ck0_ref[0]
        xbuf[8 - (CONV_W - 1):8, 2 * dk:3 * dk] = cv0_ref[0]
        s_ref[...] = s0_ref[0, 0]

    xbuf[8:8 + tb, 0:dk] = q_ref[...].astype(F32)
    xbuf[8:8 + tb, dk:2 * dk] = k_ref[...].astype(F32)
    xbuf[8:8 + tb, 2 * dk:3 * dk] = v_ref[...].astype(F32)

    def conv(idx, w_ref):
        w = w_ref[...]
        cols = slice(idx * dk, (idx + 1) * dk)
        y = xbuf[8:8 + tb, cols] * w[CONV_W - 1:CONV_W]
        for i in range(1, CONV_W):
            y = y + xbuf[8 - i:8 - i + tb, cols] * w[CONV_W - 1 - i:CONV_W - i]
        return y * _sigmoid(y)

    q = conv(0, wq_ref)
    k = conv(1, wk_ref)
    v = conv(2, wv_ref)
    xbuf[0:8, :] = xbuf[tb:tb + 8, :]

    q = q * (lax.rsqrt(jnp.sum(q * q, axis=-1, keepdims=True) + EPS) * (dk ** -0.5))
    k = k * lax.rsqrt(jnp.sum(k * k, axis=-1, keepdims=True) + EPS)

    sm = sm_ref[...]
    gp = gp_ref[...]
    beta_all = _sigmoid(sm)
    xg = sm + gp[1:2]
    softplus = jnp.maximum(xg, 0.0) + jnp.log1p(jnp.exp(-jnp.abs(xg)))
    g_all = -jnp.exp(gp[0:1]) * softplus
    lane = lax.broadcasted_iota(jnp.int32, (tb, LANE), 1)
    beta = jnp.sum(jnp.where(lane == h, beta_all, 0.0), axis=-1, keepdims=True)
    g = jnp.sum(jnp.where(lane == h + GDN_HEADS, g_all, 0.0), axis=-1, keepdims=True)

    shift = int(math.log2(c))
    row = lax.broadcasted_iota(jnp.int32, (tb, tb), 0)
    col = lax.broadcasted_iota(jnp.int32, (tb, tb), 1)
    same = jnp.right_shift(row, shift) == jnp.right_shift(col, shift)
    incl = jnp.logical_and(same, col <= row)
    strict = jnp.logical_and(same, col < row)

    lincl = jnp.where(incl, 1.0, 0.0).astype(BF16)
    g_hi = g.astype(BF16).astype(F32)
    g_lo = (g - g_hi).astype(BF16).astype(F32)
    dmat = (_dot(lincl, jnp.where(strict, g_hi, 0.0).astype(BF16))
            + _dot(lincl, jnp.where(strict, g_lo, 0.0).astype(BF16)))
    gcb = (_dot(lincl, jnp.broadcast_to(g_hi, (tb, dk)).astype(BF16))
           + _dot(lincl, jnp.broadcast_to(g_lo, (tb, dk)).astype(BF16)))
    decay = jnp.where(incl, jnp.exp(dmat), 0.0)

    kb = k * beta
    k_b = k.astype(BF16)
    kk = _nt(kb.astype(BF16), k_b)
    x = jnp.where(strict, -(kk * decay), 0.0)
    tmat = jnp.where(row == col, 1.0, 0.0) + x
    p = x
    for _ in range(shift - 1):
        pb = p.astype(BF16)
        p = _dot(pb, pb)
        tmat = tmat + _dot(tmat.astype(BF16), p.astype(BF16))
    t_b = tmat.astype(BF16)
    egc = jnp.exp(gcb)
    u = _dot(t_b, (v * beta).astype(BF16))
    w = _dot(t_b, (kb * egc).astype(BF16))
    attn = jnp.where(incl, _nt(q.astype(BF16), k_b) * decay, 0.0)
    qg = (q * egc).astype(BF16)
    w_b = w.astype(BF16)

    vnews, inters = [], []
    for ci in range(tb // c):
        rows = slice(ci * c, (ci + 1) * c)
        s = s_ref[...]
        s_b = s.astype(BF16)
        v_new = u[rows] - _dot(w_b[rows], s_b)
        inters.append(_dot(qg[rows], s_b))
        glast = gcb[(ci + 1) * c - 1:(ci + 1) * c, :]
        kdec = k[rows] * jnp.exp(glast - gcb[rows])
        s_ref[...] = s * jnp.exp(glast) + _tn(kdec.astype(BF16), v_new.astype(BF16))
        vnews.append(v_new)
    v_new = vnews[0] if len(vnews) == 1 else jnp.concatenate(vnews, axis=0)
    inter = inters[0] if len(inters) == 1 else jnp.concatenate(inters, axis=0)
    o = inter + _dot(attn.astype(BF16), v_new.astype(BF16))

    o = _rms(o, gn_ref[...])
    z = z_ref[...].astype(F32)
    o_ref[...] = (o * (z * _sigmoid(z))).astype(BF16)
    sout_ref[0, 0] = s_ref[...]


def _gdn(main, small, conv0, ssm0, w_conv, gate_rows, gnorm_w, batch, seq, tb, c):
    n_t = seq // tb
    hh = GDN_HEADS
    dk = GDN_DK

    def blk(off):
        return pl.BlockSpec((tb, dk), lambda b, h, t: (b * n_t + t, off + h))

    def cblk(off):
        return pl.BlockSpec((1, CONV_W - 1, dk), lambda b, h, t: (b, 0, off + h))

    def wblk(off):
        return pl.BlockSpec((CONV_W, dk), lambda b, h, t: (0, off + h))

    return pl.pallas_call(
        functools.partial(_gdn_kernel, tb=tb, c=c),
        grid=(batch, hh, n_t),
        in_specs=[
            blk(_QB), blk(_KB), blk(_VB), blk(_ZGB),
            pl.BlockSpec((tb, LANE), lambda b, h, t: (b * n_t + t, 1)),
            cblk(_QB), cblk(_KB), cblk(_VB),
            pl.BlockSpec((1, 1, dk, GDN_DV), lambda b, h, t: (b, h, 0, 0)),
            wblk(_QB), wblk(_KB), wblk(_VB),
            pl.BlockSpec((2, LANE), lambda b, h, t: (0, 0)),
            pl.BlockSpec((1, GDN_DV), lambda b, h, t: (0, 0)),
        ],
        out_specs=[
            pl.BlockSpec((tb, GDN_DV), lambda b, h, t: (b * n_t + t, h)),
            pl.BlockSpec((1, 1, dk, GDN_DV), lambda b, h, t: (b, h, 0, 0)),
        ],
        out_shape=[
            jax.ShapeDtypeStruct((batch * seq, hh * GDN_DV), BF16),
            jax.ShapeDtypeStruct((batch, hh, dk, GDN_DV), F32),
        ],
        scratch_shapes=[pltpu.VMEM((tb + 8, 3 * dk), F32), pltpu.VMEM((dk, GDN_DV), F32)],
        compiler_params=_params("parallel", "parallel", "arbitrary"),
        name="gdn",
    )(main, main, main, main, small, conv0, conv0, conv0, ssm0, w_conv, w_conv, w_conv, gate_rows, gnorm_w)


def _mla_prep_kernel(cq_ref, ckv_ref, sm_ref, cosq_ref, sinq_ref, cosk_ref, sink_ref, qnw_ref, kvnw_ref,
                     wqt_ref, wuk_ref, wuvt_ref, qt_ref, k_ref, vt_ref, lat_ref, kr_ref):
    scale = QK_DIM ** -0.5
    half = ROPE_DIM // 2
    cqn = _rms(cq_ref[...].astype(F32), qnw_ref[...]).astype(BF16)
    ckvn32 = _rms(ckv_ref[...].astype(F32), kvnw_ref[...])
    lat_ref[0] = ckvn32
    ckvn = ckvn32.astype(BF16)
    sm = sm_ref[...]
    krope32 = sm[:, 0:ROPE_DIM] * cosk_ref[...] + sm[:, ROPE_DIM:2 * ROPE_DIM] * sink_ref[...]
    kr_ref[0] = krope32
    krope = krope32.astype(BF16)
    knope = _dot(ckvn, wuk_ref[...])
    cq_cos = cosq_ref[...]
    cq_sin = sinq_ref[...]
    for h in range(MLA_HEADS):
        k_ref[0, h, :, 0:NOPE_DIM] = knope[:, h * NOPE_DIM:(h + 1) * NOPE_DIM].astype(BF16)
        k_ref[0, h, :, NOPE_DIM:QK_DIM] = krope
        vt_ref[0, h] = _nt(wuvt_ref[h], ckvn).astype(BF16)
        qt = _nt(wqt_ref[h], cqn)
        x1 = qt[NOPE_DIM:NOPE_DIM + half]
        x2 = qt[NOPE_DIM + half:QK_DIM]
        qt_ref[0, h, 0:NOPE_DIM, :] = (qt[0:NOPE_DIM] * scale).astype(BF16)
        qt_ref[0, h, NOPE_DIM:NOPE_DIM + half, :] = ((x1 * cq_cos - x2 * cq_sin) * scale).astype(BF16)
        qt_ref[0, h, NOPE_DIM + half:QK_DIM, :] = ((x2 * cq_cos + x1 * cq_sin) * scale).astype(BF16)


def _mla_prep(main, small, cos_t, sin_t, cos2, sin2, qn_w, kvn_w, wq_t, w_uk2, wuv_t, batch, seq, tm):
    n_t = seq // tm
    hh = MLA_HEADS
    half = ROPE_DIM // 2
    const = lambda *shape: pl.BlockSpec(shape, lambda b, i: (0,) * len(shape))
    return pl.pallas_call(
        _mla_prep_kernel,
        grid=(batch, n_t),
        in_specs=[
            pl.BlockSpec((tm, Q_LORA), lambda b, i: (b * n_t + i, _CQ_BLK)),
            pl.BlockSpec((tm, KV_LORA), lambda b, i: (b * n_t + i, _CKV_BLK)),
            pl.BlockSpec((tm, LANE), lambda b, i: (b * n_t + i, 0)),
            pl.BlockSpec((half, tm), lambda b, i: (0, i)),
            pl.BlockSpec((half, tm), lambda b, i: (0, i)),
            pl.BlockSpec((tm, ROPE_DIM), lambda b, i: (i, 0)),
            pl.BlockSpec((tm, ROPE_DIM), lambda b, i: (i, 0)),
            const(1, Q_LORA), const(1, KV_LORA),
            const(hh, QK_DIM, Q_LORA), const(KV_LORA, hh * NOPE_DIM), const(hh, V_DIM, KV_LORA),
        ],
        out_specs=[
            pl.BlockSpec((1, hh, QK_DIM, tm), lambda b, i: (b, 0, 0, i)),
            pl.BlockSpec((1, hh, tm, QK_DIM), lambda b, i: (b, 0, i, 0)),
            pl.BlockSpec((1, hh, V_DIM, tm), lambda b, i: (b, 0, 0, i)),
            pl.BlockSpec((1, tm, KV_LORA), lambda b, i: (b, i, 0)),
            pl.BlockSpec((1, tm, ROPE_DIM), lambda b, i: (b, i, 0)),
        ],
        out_shape=[
            jax.ShapeDtypeStruct((batch, hh, QK_DIM, seq), BF16),
            jax.ShapeDtypeStruct((batch, hh, seq, QK_DIM), BF16),
            jax.ShapeDtypeStruct((batch, hh, V_DIM, seq), BF16),
            jax.ShapeDtypeStruct((batch, seq, KV_LORA), F32),
            jax.ShapeDtypeStruct((batch, seq, ROPE_DIM), F32),
        ],
        compiler_params=_params("parallel", "parallel"),
        name="mla_prep",
    )(main, main, small, cos_t, sin_t, cos2, sin2, qn_w, kvn_w, wq_t, w_uk2, wuv_t)


def _flash_kernel(qt_ref, k_ref, vt_ref, z_ref, o_ref, acc_ref, m_ref, l_ref, *, tq):
    i = pl.program_id(2)
    qt = qt_ref[0, 0]
    m_ref[...] = jnp.full(m_ref.shape, -jnp.inf, F32)
    l_ref[...] = jnp.zeros(l_ref.shape, F32)
    acc_ref[...] = jnp.zeros(acc_ref.shape, F32)

    def update(s, vt):
        m_prev = m_ref[...]
        m_new = jnp.maximum(m_prev, jnp.max(s, axis=0, keepdims=True))
        alpha = jnp.exp(m_prev - m_new)
        p = jnp.exp(s - m_new)
        l_ref[...] = alpha * l_ref[...] + jnp.sum(p, axis=0, keepdims=True)
        acc_ref[...] = acc_ref[...] * alpha + _dot(vt, p.astype(BF16))
        m_ref[...] = m_new

    def body(j, carry):
        off = pl.multiple_of(j * tq, tq)
        s = _dot(k_ref[0, 0, pl.ds(off, tq), :], qt)
        update(s, vt_ref[0, 0, :, pl.ds(off, tq)])
        return carry

    lax.fori_loop(0, i, body, 0)

    off = pl.multiple_of(i * tq, tq)
    s = _dot(k_ref[0, 0, pl.ds(off, tq), :], qt)
    shift = int(math.log2(CHUNK))
    kc = jnp.right_shift(lax.broadcasted_iota(jnp.int32, (tq, tq), 0), shift)
    qc = jnp.right_shift(lax.broadcasted_iota(jnp.int32, (tq, tq), 1), shift)
    update(jnp.where(kc <= qc, s, -jnp.inf), vt_ref[0, 0, :, pl.ds(off, tq)])

    o = (acc_ref[...] / l_ref[...]).T
    z = z_ref[...].astype(F32)
    o_ref[...] = (o * (z * _sigmoid(z))).astype(BF16)


def _flash(q_t, k_all, v_t, main, batch, seq, tq):
    n_q = seq // tq
    hh = MLA_HEADS
    return pl.pallas_call(
        functools.partial(_flash_kernel, tq=tq),
        grid=(batch, hh, n_q),
        in_specs=[
            pl.BlockSpec((1, 1, QK_DIM, tq), lambda b, h, i: (b, h, 0, i)),
            pl.BlockSpec((1, 1, seq, QK_DIM), lambda b, h, i: (b, h, 0, 0)),
            pl.BlockSpec((1, 1, V_DIM, seq), lambda b, h, i: (b, h, 0, 0)),
            pl.BlockSpec((tq, V_DIM), lambda b, h, i: (b * n_q + i, _ZMB + h)),
        ],
        out_specs=pl.BlockSpec((tq, V_DIM), lambda b, h, i: (b * n_q + i, h)),
        out_shape=jax.ShapeDtypeStruct((batch * seq, hh * V_DIM), BF16),
        scratch_shapes=[pltpu.VMEM((V_DIM, tq), F32), pltpu.VMEM((1, tq), F32), pltpu.VMEM((1, tq), F32)],
        compiler_params=_params("parallel", "parallel", "arbitrary"),
        name="flash",
    )(q_t, k_all, v_t, main)


def _mla_decode_kernel(cq_ref, ckv_ref, sm_ref, z_ref, plat_ref, pkr_ref, cos_ref, sin_ref, qnw_ref, kvnw_ref,
                       wqn_ref, wqr_ref, wqrr_ref, wukt_ref, wuv_ref, o_ref, lat_ref, kr_ref, *, past, t):
    scale = QK_DIM ** -0.5
    cqn = _rms(cq_ref[...].astype(F32), qnw_ref[...]).astype(BF16)
    ckvn32 = _rms(ckv_ref[...].astype(F32), kvnw_ref[...])
    lat_ref[0] = ckvn32
    ckvn = ckvn32.astype(BF16)
    sm = sm_ref[...]
    cos2 = cos_ref[...]
    sin2 = sin_ref[...]
    krope32 = sm[:, 0:ROPE_DIM] * cos2 + sm[:, ROPE_DIM:2 * ROPE_DIM] * sin2
    kr_ref[0] = krope32
    krope = krope32.astype(BF16)

    qlats, qropes = [], []
    for h in range(MLA_HEADS):
        qn = _dot(cqn, wqn_ref[h]).astype(BF16)
        qlats.append((_dot(qn, wukt_ref[h]) * scale).astype(BF16))
        qr = _dot(cqn, wqr_ref[h]) * cos2 + _dot(cqn, wqrr_ref[h]) * sin2
        qropes.append((qr * scale).astype(BF16))
    qlat = jnp.concatenate(qlats, axis=0)
    qrope = jnp.concatenate(qropes, axis=0)

    plat = plat_ref[0].astype(BF16)
    pkr = pkr_ref[0].astype(BF16)
    s_past = _nt(qlat, plat) + _nt(qrope, pkr)
    s_new = _nt(qlat, ckvn) + _nt(qrope, krope)
    rows = MLA_HEADS * t
    qpos = past + jnp.bitwise_and(lax.broadcasted_iota(jnp.int32, (rows, t), 0), t - 1)
    kpos = past + lax.broadcasted_iota(jnp.int32, (rows, t), 1)
    shift = int(math.log2(CHUNK))
    visible = kpos < jnp.left_shift(jnp.right_shift(qpos, shift) + 1, shift)
    s_new = jnp.where(visible, s_new, -jnp.inf)
    m = jnp.maximum(jnp.max(s_past, axis=-1, keepdims=True), jnp.max(s_new, axis=-1, keepdims=True))
    p_past = jnp.exp(s_past - m)
    p_new = jnp.exp(s_new - m)
    l = jnp.sum(p_past, axis=-1, keepdims=True) + jnp.sum(p_new, axis=-1, keepdims=True)
    o_lat = (_dot(p_past.astype(BF16), plat) + _dot(p_new.astype(BF16), ckvn)) / l
    z = z_ref[...].astype(F32)
    gate = z * _sigmoid(z)
    for h in range(MLA_HEADS):
        o_h = _dot(o_lat[h * t:(h + 1) * t].astype(BF16), wuv_ref[h])
        cols = slice(h * V_DIM, (h + 1) * V_DIM)
        o_ref[:, cols] = (o_h * gate[:, cols]).astype(BF16)


def _mla_decode(main, small, past_lat, past_kr, cos2, sin2, qn_w, kvn_w, wq_n, wq_r, wq_rr, wuk_t, wuv, batch, t):
    past = past_lat.shape[1]
    hh = MLA_HEADS
    const = lambda *shape: pl.BlockSpec(shape, lambda b: (0,) * len(shape))
    return pl.pallas_call(
        functools.partial(_mla_decode_kernel, past=past, t=t),
        grid=(batch,),
        in_specs=[
            pl.BlockSpec((t, Q_LORA), lambda b: (b, _CQ_BLK)),
            pl.BlockSpec((t, KV_LORA), lambda b: (b, _CKV_BLK)),
            pl.BlockSpec((t, LANE), lambda b: (b, 0)),
            pl.BlockSpec((t, hh * V_DIM), lambda b: (b, _ZMB // hh)),
            pl.BlockSpec((1, past, KV_LORA), lambda b: (b, 0, 0)),
            pl.BlockSpec((1, past, ROPE_DIM), lambda b: (b, 0, 0)),
            const(t, ROPE_DIM), const(t, ROPE_DIM),
            const(1, Q_LORA), const(1, KV_LORA),
            const(hh, Q_LORA, NOPE_DIM), const(hh, Q_LORA, ROPE_DIM), const(hh, Q_LORA, ROPE_DIM),
            const(hh, NOPE_DIM, KV_LORA), const(hh, KV_LORA, V_DIM),
        ],
        out_specs=[
            pl.BlockSpec((t, hh * V_DIM), lambda b: (b, 0)),
            pl.BlockSpec((1, t, KV_LORA), lambda b: (b, 0, 0)),
            pl.BlockSpec((1, t, ROPE_DIM), lambda b: (b, 0, 0)),
        ],
        out_shape=[
            jax.ShapeDtypeStruct((batch * t, hh * V_DIM), BF16),
            jax.ShapeDtypeStruct((batch, t, KV_LORA), F32),
            jax.ShapeDtypeStruct((batch, t, ROPE_DIM), F32),
        ],
        compiler_params=_params("parallel"),
        name="mla_decode",
    )(main, main, small, main, past_lat, past_kr, cos2, sin2, qn_w, kvn_w, wq_n, wq_r, wq_rr, wuk_t, wuv)


def _rope_tables(p0, t):
    pos = jnp.arange(p0, p0 + t, dtype=jnp.int32)
    inv = ROPE_THETA ** (-jnp.arange(0, ROPE_DIM, 2, dtype=F32) / ROPE_DIM)
    ang = pos.astype(F32)[:, None] * inv[None, :]
    return jnp.cos(ang), jnp.sin(ang)


def _pick(n, prefs):
    for p in prefs:
        if n % p == 0:
            return p
    return n


def _layer(x, conv0, ssm0, past_lat, past_kr, lw, lnf_w):
    (ln_w, w_in, w_conv, a_log, dt_bias, gn_w, qn_w, w_q_up, kvn_w, w_uk, w_uv, w_out) = lw
    batch, t, d = x.shape
    n = batch * t
    hh = MLA_HEADS
    half = ROPE_DIM // 2
    x2d = x.reshape(n, d)

    sizes = (3 * GDN_HEADS * GDN_DK, GDN_HEADS * GDN_DV, GDN_HEADS, GDN_HEADS, Q_LORA, KV_LORA, ROPE_DIM, hh * V_DIM)
    offs = [0]
    for s in sizes:
        offs.append(offs[-1] + s)
    w_qkv, w_zg, w_bg, w_ag, w_cq, w_ckv, w_kr, w_zm = (w_in[:, offs[i]:offs[i + 1]] for i in range(8))
    w_main = jnp.concatenate([w_qkv, w_zg, w_zm, w_cq, w_ckv], axis=1).astype(BF16)
    w_small = jnp.concatenate(
        [w_kr, w_kr[:, half:], w_kr[:, :half], w_bg, w_ag,
         jnp.zeros((d, _SMALL_COLS - 2 * ROPE_DIM - 2 * GDN_HEADS), F32)], axis=1).astype(BF16)
    pad = jnp.zeros((LANE - 2 * GDN_HEADS,), F32)
    gate_rows = jnp.stack([jnp.concatenate([jnp.zeros((GDN_HEADS,), F32), a_log, pad]),
                           jnp.concatenate([jnp.zeros((GDN_HEADS,), F32), dt_bias, pad])])

    tm = _pick(n, (1024, 512, 256))
    main, small = _in_proj(x2d, ln_w.reshape(1, d), w_main, w_small, tm, 1024)

    c = min(GDN_CHUNK, t)
    tb = _pick(t, (256, 128, 64))
    o_g, new_ssm = _gdn(main, small, conv0, ssm0, w_conv, gate_rows, gn_w.reshape(1, GDN_DV), batch, t, tb, c)
    qkv_rows = main[:, :3 * GDN_HEADS * GDN_DK].reshape(batch, t, -1)[:, max(t - (CONV_W - 1), 0):].astype(F32)
    new_conv = jnp.concatenate([conv0, qkv_rows], axis=1)[:, -(CONV_W - 1):]

    p_len = past_lat.shape[1]
    cos, sin = _rope_tables(p_len, t)
    cos2 = jnp.concatenate([cos, cos], axis=1)
    sin2 = jnp.concatenate([-sin, sin], axis=1)
    qn_w2 = qn_w.reshape(1, Q_LORA)
    kvn_w2 = kvn_w.reshape(1, KV_LORA)
    if p_len == 0:
        wq_t = jnp.transpose(w_q_up, (1, 2, 0)).astype(BF16)
        w_uk2 = w_uk.reshape(KV_LORA, hh * NOPE_DIM).astype(BF16)
        wuv_t = jnp.transpose(w_uv, (1, 2, 0)).astype(BF16)
        tmq = _pick(t, (512, 256, 128))
        q_t, k_all, v_t, lat, kr = _mla_prep(main, small, cos.T, sin.T, cos2, sin2, qn_w2, kvn_w2,
                                             wq_t, w_uk2, wuv_t, batch, t, tmq)
        o_m = _flash(q_t, k_all, v_t, main, batch, t, _pick(t, (512, 256, 128)))
    else:
        wq_n = jnp.transpose(w_q_up[:, :, :NOPE_DIM], (1, 0, 2)).astype(BF16)
        wq_r = jnp.transpose(w_q_up[:, :, NOPE_DIM:], (1, 0, 2))
        wq_rr = jnp.concatenate([wq_r[..., half:], wq_r[..., :half]], axis=-1).astype(BF16)
        wuk_t = jnp.transpose(w_uk, (1, 2, 0)).astype(BF16)
        wuv = jnp.transpose(w_uv, (1, 0, 2)).astype(BF16)
        o_m, lat, kr = _mla_decode(main, small, past_lat, past_kr, cos2, sin2, qn_w2, kvn_w2,
                                   wq_n, wq_r.astype(BF16), wq_rr, wuk_t, wuv, batch, t)

    w_g = w_out[:GDN_HEADS * GDN_DV].astype(BF16)
    w_m = w_out[GDN_HEADS * GDN_DV:].astype(BF16)
    y = _out_proj(x2d, o_g, o_m, w_g, w_m, lnf_w.reshape(1, d), _pick(n, (512, 256)))
    return y.reshape(batch, t, d), new_conv, new_ssm, lat, kr


def kernel(x_prompt, x_sample, cache_mla_latent, cache_mla_krope, state_gdn_conv, state_gdn_ssm, ln_in_w, w_in, w_conv, a_log, dt_bias, gdn_norm_w, q_norm_w, w_q_up, kv_norm_w, w_uk, w_uv, w_out, ln_final_w):
    depth = w_in.shape[0]
    assert depth == 1, "the final norm is fused into the out-projection of the single layer"
    bp = x_prompt.shape[0]
    lw = (ln_in_w[0], w_in[0], w_conv[0], a_log[0], dt_bias[0], gdn_norm_w[0],
          q_norm_w[0], w_q_up[0], kv_norm_w[0], w_uk[0], w_uv[0], w_out[0])
    zc = jnp.zeros((bp, CONV_W - 1, 3 * GDN_HEADS * GDN_DK), F32)
    zs = jnp.zeros((bp, GDN_HEADS, GDN_DK, GDN_DV), F32)
    yp, c1, s1, l1, r1 = _layer(x_prompt, zc, zs, jnp.zeros((bp, 0, KV_LORA), F32),
                                jnp.zeros((bp, 0, ROPE_DIM), F32), lw, ln_final_w)
    ys, c2, s2, l2, r2 = _layer(x_sample, state_gdn_conv[0], state_gdn_ssm[0], cache_mla_latent[0],
                                cache_mla_krope[0], lw, ln_final_w)
    return (yp, ys, c1[None], s1[None], l1[None], r1[None], c2[None], s2[None], l2[None], r2[None])
```

```python
import functools
import math

import jax
import jax.numpy as jnp
from jax import lax
from jax.experimental import pallas as pl
from jax.experimental.pallas import tpu as pltpu

F32 = jnp.float32
BF16 = jnp.bfloat16

EPS = 1e-6
CHUNK = 64
GDN_CHUNK = 64
GDN_HEADS = 8
GDN_DK = 128
GDN_DV = 128
CONV_W = 4
MLA_HEADS = 8
Q_LORA = 512
KV_LORA = 512
NOPE_DIM = 128
ROPE_DIM = 64
V_DIM = 128
ROPE_THETA = 10000.0
QK_DIM = NOPE_DIM + ROPE_DIM
V_ROWS = V_DIM + 16

LANE = 128
VMEM_LIMIT = 56 * 1024 * 1024

_QB, _KB, _VB, _ZGB, _ZMB = 0, 8, 16, 24, 32
_CQ_BLK, _CKV_BLK = 10, 11
_MAIN_COLS = 6144
_SMALL_COLS = 256


def _nt(a, b):
    return lax.dot_general(a, b, (((1,), (1,)), ((), ())), preferred_element_type=F32)


def _tn(a, b):
    return lax.dot_general(a, b, (((0,), (0,)), ((), ())), preferred_element_type=F32)


def _dot(a, b):
    return jnp.dot(a, b, preferred_element_type=F32)


def _rms(x, w):
    return x * lax.rsqrt(jnp.mean(x * x, axis=-1, keepdims=True) + EPS) * w


def _sigmoid(x):
    return 1.0 / (1.0 + jnp.exp(-x))


def _params(*sem):
    return pltpu.CompilerParams(dimension_semantics=sem, vmem_limit_bytes=VMEM_LIMIT)


def _in_proj_kernel(x_ref, lnw_ref, wm_ref, ws_ref, main_ref, small_ref, h_ref, *, rows):
    @pl.when(pl.program_id(1) == 0)
    def _():
        def body(r, c):
            sl = pl.ds(pl.multiple_of(r * rows, rows), rows)
            h_ref[sl, :] = _rms(x_ref[sl, :], lnw_ref[...]).astype(BF16)
            return c
        lax.fori_loop(0, x_ref.shape[0] // rows, body, 0)
        small_ref[...] = _dot(h_ref[...], ws_ref[...])

    main_ref[...] = _dot(h_ref[...], wm_ref[...]).astype(BF16)


def _in_proj(x2d, ln_w, w_main, w_small, tm, tn):
    n, d = x2d.shape
    nm, ns = w_main.shape[1], w_small.shape[1]
    rows = min(tm, 128)
    return pl.pallas_call(
        functools.partial(_in_proj_kernel, rows=rows),
        grid=(n // tm, nm // tn),
        in_specs=[
            pl.BlockSpec((tm, d), lambda i, j: (i, 0)),
            pl.BlockSpec((1, d), lambda i, j: (0, 0)),
            pl.BlockSpec((d, tn), lambda i, j: (0, j)),
            pl.BlockSpec((d, ns), lambda i, j: (0, 0)),
        ],
        out_specs=[
            pl.BlockSpec((tm, tn), lambda i, j: (i, j)),
            pl.BlockSpec((tm, ns), lambda i, j: (i, 0)),
        ],
        out_shape=[
            jax.ShapeDtypeStruct((n, nm), BF16),
            jax.ShapeDtypeStruct((n, ns), F32),
        ],
        scratch_shapes=[pltpu.VMEM((tm, d), BF16)],
        compiler_params=_params("parallel", "arbitrary"),
        name="in_proj",
    )(x2d, ln_w, w_main, w_small)


def _out_proj_kernel(x_ref, og_ref, om_ref, wg_ref, wm_ref, lnf_ref, y_ref):
    acc = _dot(og_ref[...], wg_ref[...]) + _dot(om_ref[...], wm_ref[...])
    y_ref[...] = _rms(x_ref[...] + acc, lnf_ref[...])


def _out_proj(x2d, o_g, o_m, w_g, w_m, lnf_w, tm):
    n, d = x2d.shape
    kg, km = o_g.shape[1], o_m.shape[1]
    return pl.pallas_call(
        _out_proj_kernel,
        grid=(n // tm,),
        in_specs=[
            pl.BlockSpec((tm, d), lambda i: (i, 0)),
            pl.BlockSpec((tm, kg), lambda i: (i, 0)),
            pl.BlockSpec((tm, km), lambda i: (i, 0)),
            pl.BlockSpec((kg, d), lambda i: (0, 0)),
            pl.BlockSpec((km, d), lambda i: (0, 0)),
            pl.BlockSpec((1, d), lambda i: (0, 0)),
        ],
        out_specs=pl.BlockSpec((tm, d), lambda i: (i, 0)),
        out_shape=jax.ShapeDtypeStruct((n, d), F32),
        compiler_params=_params("parallel"),
        name="out_proj",
    )(x2d, o_g, o_m, w_g, w_m, lnf_w)


def _gdn_kernel(q_ref, k_ref, v_ref, z_ref, sm_ref, cq0_ref, ck0_ref, cv0_ref, s0_ref,
                wq_ref, wk_ref, wv_ref, gp_ref, gn_ref, o_ref, sout_ref, xbuf, s_ref, *, tb, c, hb):
    hg = pl.program_id(1)
    t = pl.program_id(2)
    dk = GDN_DK
    wd = hb * dk

    @pl.when(t == 0)
    def _():
        xbuf[0:8, :] = jnp.zeros((8, 3 * wd), F32)
        xbuf[8 - (CONV_W - 1):8, 0:wd] = cq0_ref[0]
        xbuf[8 - (CONV_W - 1):8, wd:2 * wd] = ck0_ref[0]
        xbuf[8 - (CONV_W - 1):8, 2 * wd:3 * wd] = cv0_ref[0]
        s_ref[...] = s0_ref[0]

    xbuf[8:8 + tb, 0:wd] = q_ref[...].astype(F32)
    xbuf[8:8 + tb, wd:2 * wd] = k_ref[...].astype(F32)
    xbuf[8:8 + tb, 2 * wd:3 * wd] = v_ref[...].astype(F32)

    def conv(idx, hh, w_ref):
        w = w_ref[:, hh * dk:(hh + 1) * dk]
        cols = slice(idx * wd + hh * dk, idx * wd + (hh + 1) * dk)
        y = xbuf[8:8 + tb, cols] * w[CONV_W - 1:CONV_W]
        for i in range(1, CONV_W):
            y = y + xbuf[8 - i:8 - i + tb, cols] * w[CONV_W - 1 - i:CONV_W - i]
        return y * _sigmoid(y)

    sm = sm_ref[...]
    gp = gp_ref[...]
    beta_all = _sigmoid(sm)
    xg = sm + gp[1:2]
    softplus = jnp.maximum(xg, 0.0) + jnp.log1p(jnp.exp(-jnp.abs(xg)))
    g_all = -jnp.exp(gp[0:1]) * softplus
    lane = lax.broadcasted_iota(jnp.int32, (tb, LANE), 1)

    shift = int(math.log2(c))
    row = lax.broadcasted_iota(jnp.int32, (tb, tb), 0)
    col = lax.broadcasted_iota(jnp.int32, (tb, tb), 1)
    same = jnp.right_shift(row, shift) == jnp.right_shift(col, shift)
    incl = jnp.logical_and(same, col <= row)
    strict = jnp.logical_and(same, col < row)
    lincl = jnp.where(incl, 1.0, 0.0).astype(BF16)
    eye = jnp.where(row == col, 1.0, 0.0)

    hs = range(hb)
    q = [conv(0, h, wq_ref) for h in hs]
    k = [conv(1, h, wk_ref) for h in hs]
    v = [conv(2, h, wv_ref) for h in hs]
    q = [x * (lax.rsqrt(jnp.sum(x * x, axis=-1, keepdims=True) + EPS) * (dk ** -0.5)) for x in q]
    k = [x * lax.rsqrt(jnp.sum(x * x, axis=-1, keepdims=True) + EPS) for x in k]
    beta = [jnp.sum(jnp.where(lane == hg * hb + h, beta_all, 0.0), axis=-1, keepdims=True) for h in hs]
    g = [jnp.sum(jnp.where(lane == hg * hb + h + GDN_HEADS, g_all, 0.0), axis=-1, keepdims=True) for h in hs]

    g_hi = [x.astype(BF16).astype(F32) for x in g]
    g_lo = [(x - y).astype(BF16).astype(F32) for x, y in zip(g, g_hi)]
    dmat = [_dot(lincl, jnp.where(strict, x, 0.0).astype(BF16)) + _dot(lincl, jnp.where(strict, y, 0.0).astype(BF16))
            for x, y in zip(g_hi, g_lo)]
    gcb = [_dot(lincl, jnp.broadcast_to(x, (tb, dk)).astype(BF16))
           + _dot(lincl, jnp.broadcast_to(y, (tb, dk)).astype(BF16)) for x, y in zip(g_hi, g_lo)]
    decay = [jnp.where(incl, jnp.exp(x), 0.0) for x in dmat]

    kb = [x * y for x, y in zip(k, beta)]
    k_b = [x.astype(BF16) for x in k]
    kk = [_nt(x.astype(BF16), y) for x, y in zip(kb, k_b)]
    p = [jnp.where(strict, -(x * y), 0.0) for x, y in zip(kk, decay)]
    tmat = [eye + x for x in p]
    for _ in range(shift - 1):
        pb = [x.astype(BF16) for x in p]
        p = [_dot(x, x) for x in pb]
        tmat = [x + _dot(x.astype(BF16), y.astype(BF16)) for x, y in zip(tmat, p)]
    t_b = [x.astype(BF16) for x in tmat]
    egc = [jnp.exp(x) for x in gcb]
    u = [_dot(x, (y * z).astype(BF16)) for x, y, z in zip(t_b, v, beta)]
    w_b = [_dot(x, (y * z).astype(BF16)).astype(BF16) for x, y, z in zip(t_b, kb, egc)]
    attn = [jnp.where(incl, _nt(x.astype(BF16), y) * z, 0.0).astype(BF16) for x, y, z in zip(q, k_b, decay)]
    qg = [(x * y).astype(BF16) for x, y in zip(q, egc)]

    s = [s_ref[h] for h in hs]
    vnews = [[] for _ in hs]
    inters = [[] for _ in hs]
    for ci in range(tb // c):
        rows = slice(ci * c, (ci + 1) * c)
        s_b = [x.astype(BF16) for x in s]
        v_new = [u[h][rows] - _dot(w_b[h][rows], s_b[h]) for h in hs]
        for h in hs:
            inters[h].append(_dot(qg[h][rows], s_b[h]))
            vnews[h].append(v_new[h])
        glast = [x[(ci + 1) * c - 1:(ci + 1) * c, :] for x in gcb]
        kdec = [k[h][rows] * jnp.exp(glast[h] - gcb[h][rows]) for h in hs]
        s = [s[h] * jnp.exp(glast[h]) + _tn(kdec[h].astype(BF16), v_new[h].astype(BF16)) for h in hs]
    for h in hs:
        s_ref[h] = s[h]
        sout_ref[0, h] = s[h]
    cat = lambda xs: xs[0] if len(xs) == 1 else jnp.concatenate(xs, axis=0)
    o = [cat(inters[h]) + _dot(attn[h], cat(vnews[h]).astype(BF16)) for h in hs]
    for h in hs:
        z = z_ref[:, h * dk:(h + 1) * dk].astype(F32)
        o_ref[:, h * dk:(h + 1) * dk] = (_rms(o[h], gn_ref[...]) * (z * _sigmoid(z))).astype(BF16)

    xbuf[0:8, :] = xbuf[tb:tb + 8, :]


def _gdn(main, small, conv0, ssm0, w_conv, gate_rows, gnorm_w, batch, seq, tb, c, hb):
    n_t = seq // tb
    hh = GDN_HEADS
    dk = GDN_DK
    wd = hb * dk

    def blk(off):
        return pl.BlockSpec((tb, wd), lambda b, h, t: (b * n_t + t, off // hb + h))

    def cblk(off):
        return pl.BlockSpec((1, CONV_W - 1, wd), lambda b, h, t: (b, 0, off // hb + h))

    def wblk(off):
        return pl.BlockSpec((CONV_W, wd), lambda b, h, t: (0, off // hb + h))

    return pl.pallas_call(
        functools.partial(_gdn_kernel, tb=tb, c=c, hb=hb),
        grid=(batch, hh // hb, n_t),
        in_specs=[
            blk(_QB), blk(_KB), blk(_VB), blk(_ZGB),
            pl.BlockSpec((tb, LANE), lambda b, h, t: (b * n_t + t, 1)),
            cblk(_QB), cblk(_KB), cblk(_VB),
            pl.BlockSpec((1, hb, dk, GDN_DV), lambda b, h, t: (b, h, 0, 0)),
            wblk(_QB), wblk(_KB), wblk(_VB),
            pl.BlockSpec((2, LANE), lambda b, h, t: (0, 0)),
            pl.BlockSpec((1, GDN_DV), lambda b, h, t: (0, 0)),
        ],
        out_specs=[
            pl.BlockSpec((tb, wd), lambda b, h, t: (b * n_t + t, h)),
            pl.BlockSpec((1, hb, dk, GDN_DV), lambda b, h, t: (b, h, 0, 0)),
        ],
        out_shape=[
            jax.ShapeDtypeStruct((batch * seq, hh * GDN_DV), BF16),
            jax.ShapeDtypeStruct((batch, hh, dk, GDN_DV), F32),
        ],
        scratch_shapes=[pltpu.VMEM((tb + 8, 3 * wd), F32), pltpu.VMEM((hb, dk, GDN_DV), F32)],
        compiler_params=_params("parallel", "parallel", "arbitrary"),
        name="gdn",
    )(main, main, main, main, small, conv0, conv0, conv0, ssm0, w_conv, w_conv, w_conv, gate_rows, gnorm_w)


def _mla_prep_kernel(cq_ref, ckv_ref, sm_ref, cosq_ref, sinq_ref, cosk_ref, sink_ref, qnw_ref, kvnw_ref,
                     wqt_ref, wuk_ref, wuvt_ref, qt_ref, k_ref, vt_ref, lat_ref, kr_ref):
    scale = QK_DIM ** -0.5 * math.log2(math.e)
    half = ROPE_DIM // 2
    tm = cq_ref.shape[0]
    cqn = _rms(cq_ref[...].astype(F32), qnw_ref[...]).astype(BF16)
    ckvn32 = _rms(ckv_ref[...].astype(F32), kvnw_ref[...])
    lat_ref[0] = ckvn32
    ckvn = ckvn32.astype(BF16)
    sm = sm_ref[...]
    krope32 = sm[:, 0:ROPE_DIM] * cosk_ref[...] + sm[:, ROPE_DIM:2 * ROPE_DIM] * sink_ref[...]
    kr_ref[0] = krope32
    krope = krope32.astype(BF16)
    knope = _dot(ckvn, wuk_ref[...])
    cq_cos = cosq_ref[...]
    cq_sin = sinq_ref[...]
    for h in range(MLA_HEADS):
        k_ref[0, h, :, 0:NOPE_DIM] = knope[:, h * NOPE_DIM:(h + 1) * NOPE_DIM].astype(BF16)
        k_ref[0, h, :, NOPE_DIM:QK_DIM] = krope
        vt_ref[0, h, 0:V_DIM, :] = _nt(wuvt_ref[h], ckvn).astype(BF16)
        vt_ref[0, h, V_DIM:V_ROWS, :] = jnp.ones((V_ROWS - V_DIM, tm), BF16)
        qt = _nt(wqt_ref[h], cqn)
        x1 = qt[NOPE_DIM:NOPE_DIM + half]
        x2 = qt[NOPE_DIM + half:QK_DIM]
        qt_ref[0, h, 0:NOPE_DIM, :] = (qt[0:NOPE_DIM] * scale).astype(BF16)
        qt_ref[0, h, NOPE_DIM:NOPE_DIM + half, :] = ((x1 * cq_cos - x2 * cq_sin) * scale).astype(BF16)
        qt_ref[0, h, NOPE_DIM + half:QK_DIM, :] = ((x2 * cq_cos + x1 * cq_sin) * scale).astype(BF16)


def _mla_prep(main, small, cos_t, sin_t, cos2, sin2, qn_w, kvn_w, wq_t, w_uk2, wuv_t, batch, seq, tm):
    n_t = seq // tm
    hh = MLA_HEADS
    half = ROPE_DIM // 2
    const = lambda *shape: pl.BlockSpec(shape, lambda b, i: (0,) * len(shape))
    return pl.pallas_call(
        _mla_prep_kernel,
        grid=(batch, n_t),
        in_specs=[
            pl.BlockSpec((tm, Q_LORA), lambda b, i: (b * n_t + i, _CQ_BLK)),
            pl.BlockSpec((tm, KV_LORA), lambda b, i: (b * n_t + i, _CKV_BLK)),
            pl.BlockSpec((tm, LANE), lambda b, i: (b * n_t + i, 0)),
            pl.BlockSpec((half, tm), lambda b, i: (0, i)),
            pl.BlockSpec((half, tm), lambda b, i: (0, i)),
            pl.BlockSpec((tm, ROPE_DIM), lambda b, i: (i, 0)),
            pl.BlockSpec((tm, ROPE_DIM), lambda b, i: (i, 0)),
            const(1, Q_LORA), const(1, KV_LORA),
            const(hh, QK_DIM, Q_LORA), const(KV_LORA, hh * NOPE_DIM), const(hh, V_DIM, KV_LORA),
        ],
        out_specs=[
            pl.BlockSpec((1, hh, QK_DIM, tm), lambda b, i: (b, 0, 0, i)),
            pl.BlockSpec((1, hh, tm, QK_DIM), lambda b, i: (b, 0, i, 0)),
            pl.BlockSpec((1, hh, V_ROWS, tm), lambda b, i: (b, 0, 0, i)),
            pl.BlockSpec((1, tm, KV_LORA), lambda b, i: (b, i, 0)),
            pl.BlockSpec((1, tm, ROPE_DIM), lambda b, i: (b, i, 0)),
        ],
        out_shape=[
            jax.ShapeDtypeStruct((batch, hh, QK_DIM, seq), BF16),
            jax.ShapeDtypeStruct((batch, hh, seq, QK_DIM), BF16),
            jax.ShapeDtypeStruct((batch, hh, V_ROWS, seq), BF16),
            jax.ShapeDtypeStruct((batch, seq, KV_LORA), F32),
            jax.ShapeDtypeStruct((batch, seq, ROPE_DIM), F32),
        ],
        compiler_params=_params("parallel", "parallel"),
        name="mla_prep",
    )(main, main, small, cos_t, sin_t, cos2, sin2, qn_w, kvn_w, wq_t, w_uk2, wuv_t)


def _flash_kernel(qt_ref, k_ref, vt_ref, z_ref, o_ref, acc_ref, m_ref, *, tq, tk, ts):
    i = pl.program_id(2)
    m_ref[...] = jnp.full(m_ref.shape, -jnp.inf, F32)
    acc_ref[...] = jnp.zeros(acc_ref.shape, F32)
    shift = int(math.log2(CHUNK))

    def update(j, diag):
        off = pl.multiple_of(j * tk, tk)
        k = k_ref[0, 0, pl.ds(off, tk), :]
        vt = vt_ref[0, 0, :, pl.ds(off, tk)]
        for c in range(tq // ts):
            cols = slice(c * ts, (c + 1) * ts)
            s = _dot(k, qt_ref[0, 0, :, cols])
            if diag:
                kc = jnp.right_shift(lax.broadcasted_iota(jnp.int32, (tk, ts), 0), shift)
                qc = jnp.right_shift(lax.broadcasted_iota(jnp.int32, (tk, ts), 1) + c * ts, shift)
                s = jnp.where(kc <= qc, s, -jnp.inf)
            m_prev = m_ref[:, cols]
            m_new = jnp.maximum(m_prev, jnp.max(s, axis=0, keepdims=True))
            alpha = jnp.exp2(m_prev - m_new)
            p = jnp.exp2(s - m_new).astype(BF16)
            acc_ref[:, cols] = acc_ref[:, cols] * alpha + _dot(vt, p)
            m_ref[:, cols] = m_new

    def body(j, carry):
        update(j, False)
        return carry

    lax.fori_loop(0, i * (tq // tk), body, 0)
    for jj in range(tq // tk):
        update(i * (tq // tk) + jj, True)

    o = (acc_ref[0:V_DIM, :] / acc_ref[V_DIM:V_DIM + 1, :]).T
    z = z_ref[...].astype(F32)
    o_ref[...] = (o * (z * _sigmoid(z))).astype(BF16)


def _flash(q_t, k_all, v_t, main, batch, seq, tq, tk, ts):
    n_q = seq // tq
    hh = MLA_HEADS
    return pl.pallas_call(
        functools.partial(_flash_kernel, tq=tq, tk=tk, ts=ts),
        grid=(batch, hh, n_q),
        in_specs=[
            pl.BlockSpec((1, 1, QK_DIM, tq), lambda b, h, i: (b, h, 0, i)),
            pl.BlockSpec((1, 1, seq, QK_DIM), lambda b, h, i: (b, h, 0, 0)),
            pl.BlockSpec((1, 1, V_ROWS, seq), lambda b, h, i: (b, h, 0, 0)),
            pl.BlockSpec((tq, V_DIM), lambda b, h, i: (b * n_q + i, _ZMB + h)),
        ],
        out_specs=pl.BlockSpec((tq, V_DIM), lambda b, h, i: (b * n_q + i, h)),
        out_shape=jax.ShapeDtypeStruct((batch * seq, hh * V_DIM), BF16),
        scratch_shapes=[pltpu.VMEM((V_ROWS, tq), F32), pltpu.VMEM((1, tq), F32)],
        compiler_params=_params("parallel", "parallel", "arbitrary"),
        name="flash",
    )(q_t, k_all, v_t, main)


def _mla_decode_kernel(cq_ref, ckv_ref, sm_ref, z_ref, plat_ref, pkr_ref, cos_ref, sin_ref, qnw_ref, kvnw_ref,
                       wqn_ref, wqr_ref, wqrr_ref, wukt_ref, wuv_ref, o_ref, lat_ref, kr_ref, *, past, t):
    scale = QK_DIM ** -0.5
    cqn = _rms(cq_ref[...].astype(F32), qnw_ref[...]).astype(BF16)
    ckvn32 = _rms(ckv_ref[...].astype(F32), kvnw_ref[...])
    lat_ref[0] = ckvn32
    ckvn = ckvn32.astype(BF16)
    sm = sm_ref[...]
    cos2 = cos_ref[...]
    sin2 = sin_ref[...]
    krope32 = sm[:, 0:ROPE_DIM] * cos2 + sm[:, ROPE_DIM:2 * ROPE_DIM] * sin2
    kr_ref[0] = krope32
    krope = krope32.astype(BF16)

    qlats, qropes = [], []
    for h in range(MLA_HEADS):
        qn = _dot(cqn, wqn_ref[h]).astype(BF16)
        qlats.append((_dot(qn, wukt_ref[h]) * scale).astype(BF16))
        qr = _dot(cqn, wqr_ref[h]) * cos2 + _dot(cqn, wqrr_ref[h]) * sin2
        qropes.append((qr * scale).astype(BF16))
    qlat = jnp.concatenate(qlats, axis=0)
    qrope = jnp.concatenate(qropes, axis=0)

    plat = plat_ref[0].astype(BF16)
    pkr = pkr_ref[0].astype(BF16)
    s_past = _nt(qlat, plat) + _nt(qrope, pkr)
    s_new = _nt(qlat, ckvn) + _nt(qrope, krope)
    rows = MLA_HEADS * t
    qpos = past + jnp.bitwise_and(lax.broadcasted_iota(jnp.int32, (rows, t), 0), t - 1)
    kpos = past + lax.broadcasted_iota(jnp.int32, (rows, t), 1)
    shift = int(math.log2(CHUNK))
    visible = kpos < jnp.left_shift(jnp.right_shift(qpos, shift) + 1, shift)
    s_new = jnp.where(visible, s_new, -jnp.inf)
    m = jnp.maximum(jnp.max(s_past, axis=-1, keepdims=True), jnp.max(s_new, axis=-1, keepdims=True))
    p_past = jnp.exp(s_past - m)
    p_new = jnp.exp(s_new - m)
    l = jnp.sum(p_past, axis=-1, keepdims=True) + jnp.sum(p_new, axis=-1, keepdims=True)
    o_lat = (_dot(p_past.astype(BF16), plat) + _dot(p_new.astype(BF16), ckvn)) / l
    z = z_ref[...].astype(F32)
    gate = z * _sigmoid(z)
    for h in range(MLA_HEADS):
        o_h = _dot(o_lat[h * t:(h + 1) * t].astype(BF16), wuv_ref[h])
        cols = slice(h * V_DIM, (h + 1) * V_DIM)
        o_ref[:, cols] = (o_h * gate[:, cols]).astype(BF16)


def _mla_decode(main, small, past_lat, past_kr, cos2, sin2, qn_w, kvn_w, wq_n, wq_r, wq_rr, wuk_t, wuv, batch, t):
    past = past_lat.shape[1]
    hh = MLA_HEADS
    const = lambda *shape: pl.BlockSpec(shape, lambda b: (0,) * len(shape))
    return pl.pallas_call(
        functools.partial(_mla_decode_kernel, past=past, t=t),
        grid=(batch,),
        in_specs=[
            pl.BlockSpec((t, Q_LORA), lambda b: (b, _CQ_BLK)),
            pl.BlockSpec((t, KV_LORA), lambda b: (b, _CKV_BLK)),
            pl.BlockSpec((t, LANE), lambda b: (b, 0)),
            pl.BlockSpec((t, hh * V_DIM), lambda b: (b, _ZMB // hh)),
            pl.BlockSpec((1, past, KV_LORA), lambda b: (b, 0, 0)),
            pl.BlockSpec((1, past, ROPE_DIM), lambda b: (b, 0, 0)),
            const(t, ROPE_DIM), const(t, ROPE_DIM),
            const(1, Q_LORA), const(1, KV_LORA),
            const(hh, Q_LORA, NOPE_DIM), const(hh, Q_LORA, ROPE_DIM), const(hh, Q_LORA, ROPE_DIM),
            const(hh, NOPE_DIM, KV_LORA), const(hh, KV_LORA, V_DIM),
        ],
        out_specs=[
            pl.BlockSpec((t, hh * V_DIM), lambda b: (b, 0)),
            pl.BlockSpec((1, t, KV_LORA), lambda b: (b, 0, 0)),
            pl.BlockSpec((1, t, ROPE_DIM), lambda b: (b, 0, 0)),
        ],
        out_shape=[
            jax.ShapeDtypeStruct((batch * t, hh * V_DIM), BF16),
            jax.ShapeDtypeStruct((batch, t, KV_LORA), F32),
            jax.ShapeDtypeStruct((batch, t, ROPE_DIM), F32),
        ],
        compiler_params=_params("parallel"),
        name="mla_decode",
    )(main, main, small, main, past_lat, past_kr, cos2, sin2, qn_w, kvn_w, wq_n, wq_r, wq_rr, wuk_t, wuv)


def _rope_tables(p0, t):
    pos = jnp.arange(p0, p0 + t, dtype=jnp.int32)
    inv = ROPE_THETA ** (-jnp.arange(0, ROPE_DIM, 2, dtype=F32) / ROPE_DIM)
    ang = pos.astype(F32)[:, None] * inv[None, :]
    return jnp.cos(ang), jnp.sin(ang)


def _pick(n, prefs):
    for p in prefs:
        if n % p == 0:
            return p
    return n


def _layer(x, conv0, ssm0, past_lat, past_kr, lw, lnf_w):
    (ln_w, w_in, w_conv, a_log, dt_bias, gn_w, qn_w, w_q_up, kvn_w, w_uk, w_uv, w_out) = lw
    batch, t, d = x.shape
    n = batch * t
    hh = MLA_HEADS
    half = ROPE_DIM // 2
    x2d = x.reshape(n, d)

    sizes = (3 * GDN_HEADS * GDN_DK, GDN_HEADS * GDN_DV, GDN_HEADS, GDN_HEADS, Q_LORA, KV_LORA, ROPE_DIM, hh * V_DIM)
    offs = [0]
    for s in sizes:
        offs.append(offs[-1] + s)
    w_qkv, w_zg, w_bg, w_ag, w_cq, w_ckv, w_kr, w_zm = (w_in[:, offs[i]:offs[i + 1]] for i in range(8))
    w_main = jnp.concatenate([w_qkv, w_zg, w_zm, w_cq, w_ckv], axis=1).astype(BF16)
    w_small = jnp.concatenate(
        [w_kr, w_kr[:, half:], w_kr[:, :half], w_bg, w_ag,
         jnp.zeros((d, _SMALL_COLS - 2 * ROPE_DIM - 2 * GDN_HEADS), F32)], axis=1).astype(BF16)
    pad = jnp.zeros((LANE - 2 * GDN_HEADS,), F32)
    gate_rows = jnp.stack([jnp.concatenate([jnp.zeros((GDN_HEADS,), F32), a_log, pad]),
                           jnp.concatenate([jnp.zeros((GDN_HEADS,), F32), dt_bias, pad])])

    tm = _pick(n, (1024, 512, 256))
    main, small = _in_proj(x2d, ln_w.reshape(1, d), w_main, w_small, tm, 1024)

    c = min(GDN_CHUNK, t)
    tb = _pick(t, (256, 128, 64))
    o_g, new_ssm = _gdn(main, small, conv0, ssm0, w_conv, gate_rows, gn_w.reshape(1, GDN_DV), batch, t, tb, c, 8)
    qkv_rows = main.reshape(batch, t, -1)[:, max(t - (CONV_W - 1), 0):, :3 * GDN_HEADS * GDN_DK].astype(F32)
    new_conv = jnp.concatenate([conv0, qkv_rows], axis=1)[:, -(CONV_W - 1):]

    p_len = past_lat.shape[1]
    cos, sin = _rope_tables(p_len, t)
    cos2 = jnp.concatenate([cos, cos], axis=1)
    sin2 = jnp.concatenate([-sin, sin], axis=1)
    qn_w2 = qn_w.reshape(1, Q_LORA)
    kvn_w2 = kvn_w.reshape(1, KV_LORA)
    if p_len == 0:
        wq_t = jnp.transpose(w_q_up, (1, 2, 0)).astype(BF16)
        w_uk2 = w_uk.reshape(KV_LORA, hh * NOPE_DIM).astype(BF16)
        wuv_t = jnp.transpose(w_uv, (1, 2, 0)).astype(BF16)
        tmq = _pick(t, (512, 256, 128))
        q_t, k_all, v_t, lat, kr = _mla_prep(main, small, cos.T, sin.T, cos2, sin2, qn_w2, kvn_w2,
                                             wq_t, w_uk2, wuv_t, batch, t, tmq)
        tq = _pick(t, (512, 256, 128))
        o_m = _flash(q_t, k_all, v_t, main, batch, t, tq, tq, tq)
    else:
        wq_n = jnp.transpose(w_q_up[:, :, :NOPE_DIM], (1, 0, 2)).astype(BF16)
        wq_r = jnp.transpose(w_q_up[:, :, NOPE_DIM:], (1, 0, 2))
        wq_rr = jnp.concatenate([wq_r[..., half:], wq_r[..., :half]], axis=-1).astype(BF16)
        wuk_t = jnp.transpose(w_uk, (1, 2, 0)).astype(BF16)
        wuv = jnp.transpose(w_uv, (1, 0, 2)).astype(BF16)
        o_m, lat, kr = _mla_decode(main, small, past_lat, past_kr, cos2, sin2, qn_w2, kvn_w2,
                                   wq_n, wq_r.astype(BF16), wq_rr, wuk_t, wuv, batch, t)

    w_g = w_out[:GDN_HEADS * GDN_DV].astype(BF16)
    w_m = w_out[GDN_HEADS * GDN_DV:].astype(BF16)
    y = _out_proj(x2d, o_g, o_m, w_g, w_m, lnf_w.reshape(1, d), _pick(n, (512, 256)))
    return y.reshape(batch, t, d), new_conv, new_ssm, lat, kr


def kernel(x_prompt, x_sample, cache_mla_latent, cache_mla_krope, state_gdn_conv, state_gdn_ssm, ln_in_w, w_in, w_conv, a_log, dt_bias, gdn_norm_w, q_norm_w, w_q_up, kv_norm_w, w_uk, w_uv, w_out, ln_final_w):
    depth = w_in.shape[0]
    assert depth == 1, "the final norm is fused into the out-projection of the single layer"
    bp = x_prompt.shape[0]
    lw = (ln_in_w[0], w_in[0], w_conv[0], a_log[0], dt_bias[0], gdn_norm_w[0],
          q_norm_w[0], w_q_up[0], kv_norm_w[0], w_uk[0], w_uv[0], w_out[0])
    zc = jnp.zeros((bp, CONV_W - 1, 3 * GDN_HEADS * GDN_DK), F32)
    zs = jnp.zeros((bp, GDN_HEADS, GDN_DK, GDN_DV), F32)
    yp, c1, s1, l1, r1 = _layer(x_prompt, zc, zs, jnp.zeros((bp, 0, KV_LORA), F32),
                                jnp.zeros((bp, 0, ROPE_DIM), F32), lw, ln_final_w)
    ys, c2, s2, l2, r2 = _layer(x_sample, state_gdn_conv[0], state_gdn_ssm[0], cache_mla_latent[0],
                                cache_mla_krope[0], lw, ln_final_w)
    return (yp, ys, c1[None], s1[None], l1[None], r1[None], c2[None], s2[None], l2[None], r2[None])
```

```python
import functools
import math

import jax
import jax.numpy as jnp
from jax import lax
from jax.experimental import pallas as pl
from jax.experimental.pallas import tpu as pltpu

F32 = jnp.float32
BF16 = jnp.bfloat16

EPS = 1e-6
CHUNK = 64
GDN_CHUNK = 64
GDN_HEADS = 8
GDN_DK = 128
GDN_DV = 128
CONV_W = 4
MLA_HEADS = 8
Q_LORA = 512
KV_LORA = 512
NOPE_DIM = 128
ROPE_DIM = 64
V_DIM = 128
ROPE_THETA = 10000.0
QK_DIM = NOPE_DIM + ROPE_DIM
V_ROWS = V_DIM + 16

LANE = 128
VMEM_LIMIT = 56 * 1024 * 1024

_QB, _KB, _VB, _ZGB, _ZMB = 0, 8, 16, 24, 32
_CQ_BLK, _CKV_BLK = 10, 11
_MAIN_COLS = 6144
_SMALL_COLS = 256


def _nt(a, b):
    return lax.dot_general(a, b, (((1,), (1,)), ((), ())), preferred_element_type=F32)


def _tn(a, b):
    return lax.dot_general(a, b, (((0,), (0,)), ((), ())), preferred_element_type=F32)


def _dot(a, b):
    return jnp.dot(a, b, preferred_element_type=F32)


def _rms(x, w):
    return x * lax.rsqrt(jnp.mean(x * x, axis=-1, keepdims=True) + EPS) * w


def _sigmoid(x):
    return 1.0 / (1.0 + jnp.exp(-x))


def _params(*sem):
    return pltpu.CompilerParams(dimension_semantics=sem, vmem_limit_bytes=VMEM_LIMIT)


def _in_proj_kernel(x_ref, lnw_ref, wm_ref, ws_ref, main_ref, small_ref, h_ref, *, rows):
    @pl.when(pl.program_id(1) == 0)
    def _():
        def body(r, c):
            sl = pl.ds(pl.multiple_of(r * rows, rows), rows)
            h_ref[sl, :] = _rms(x_ref[sl, :], lnw_ref[...]).astype(BF16)
            return c
        lax.fori_loop(0, x_ref.shape[0] // rows, body, 0)
        small_ref[...] = _dot(h_ref[...], ws_ref[...])

    main_ref[...] = _dot(h_ref[...], wm_ref[...]).astype(BF16)


def _in_proj(x2d, ln_w, w_main, w_small, tm, tn):
    n, d = x2d.shape
    nm, ns = w_main.shape[1], w_small.shape[1]
    rows = min(tm, 128)
    return pl.pallas_call(
        functools.partial(_in_proj_kernel, rows=rows),
        grid=(n // tm, nm // tn),
        in_specs=[
            pl.BlockSpec((tm, d), lambda i, j: (i, 0)),
            pl.BlockSpec((1, d), lambda i, j: (0, 0)),
            pl.BlockSpec((d, tn), lambda i, j: (0, j)),
            pl.BlockSpec((d, ns), lambda i, j: (0, 0)),
        ],
        out_specs=[
            pl.BlockSpec((tm, tn), lambda i, j: (i, j)),
            pl.BlockSpec((tm, ns), lambda i, j: (i, 0)),
        ],
        out_shape=[
            jax.ShapeDtypeStruct((n, nm), BF16),
            jax.ShapeDtypeStruct((n, ns), F32),
        ],
        scratch_shapes=[pltpu.VMEM((tm, d), BF16)],
        compiler_params=_params("parallel", "arbitrary"),
        name="in_proj",
    )(x2d, ln_w, w_main, w_small)


def _out_proj_kernel(x_ref, og_ref, om_ref, wg_ref, wm_ref, lnf_ref, y_ref):
    acc = _dot(og_ref[...], wg_ref[...]) + _dot(om_ref[...], wm_ref[...])
    y_ref[...] = _rms(x_ref[...] + acc, lnf_ref[...])


def _out_proj(x2d, o_g, o_m, w_g, w_m, lnf_w, tm):
    n, d = x2d.shape
    kg, km = o_g.shape[1], o_m.shape[1]
    return pl.pallas_call(
        _out_proj_kernel,
        grid=(n // tm,),
        in_specs=[
            pl.BlockSpec((tm, d), lambda i: (i, 0)),
            pl.BlockSpec((tm, kg), lambda i: (i, 0)),
            pl.BlockSpec((tm, km), lambda i: (i, 0)),
            pl.BlockSpec((kg, d), lambda i: (0, 0)),
            pl.BlockSpec((km, d), lambda i: (0, 0)),
            pl.BlockSpec((1, d), lambda i: (0, 0)),
        ],
        out_specs=pl.BlockSpec((tm, d), lambda i: (i, 0)),
        out_shape=jax.ShapeDtypeStruct((n, d), F32),
        compiler_params=_params("parallel"),
        name="out_proj",
    )(x2d, o_g, o_m, w_g, w_m, lnf_w)


def _gdn_kernel(q_ref, k_ref, v_ref, z_ref, sm_ref, cq0_ref, ck0_ref, cv0_ref, s0_ref,
                wq_ref, wk_ref, wv_ref, gp_ref, gn_ref, o_ref, sout_ref, xbuf, s_ref, *, tb, c, hb):
    hg = pl.program_id(1)
    t = pl.program_id(2)
    dk = GDN_DK
    wd = hb * dk

    @pl.when(t == 0)
    def _():
        xbuf[0:8, :] = jnp.zeros((8, 3 * wd), F32)
        xbuf[8 - (CONV_W - 1):8, 0:wd] = cq0_ref[0]
        xbuf[8 - (CONV_W - 1):8, wd:2 * wd] = ck0_ref[0]
        xbuf[8 - (CONV_W - 1):8, 2 * wd:3 * wd] = cv0_ref[0]
        s_ref[...] = s0_ref[0]

    xbuf[8:8 + tb, 0:wd] = q_ref[...].astype(F32)
    xbuf[8:8 + tb, wd:2 * wd] = k_ref[...].astype(F32)
    xbuf[8:8 + tb, 2 * wd:3 * wd] = v_ref[...].astype(F32)

    def conv(idx, hh, w_ref):
        w = w_ref[:, hh * dk:(hh + 1) * dk]
        cols = slice(idx * wd + hh * dk, idx * wd + (hh + 1) * dk)
        y = xbuf[8:8 + tb, cols] * w[CONV_W - 1:CONV_W]
        for i in range(1, CONV_W):
            y = y + xbuf[8 - i:8 - i + tb, cols] * w[CONV_W - 1 - i:CONV_W - i]
        return y * _sigmoid(y)

    sm = sm_ref[...]
    gp = gp_ref[...]
    beta_all = _sigmoid(sm)
    xg = sm + gp[1:2]
    softplus = jnp.maximum(xg, 0.0) + jnp.log1p(jnp.exp(-jnp.abs(xg)))
    g_all = -jnp.exp(gp[0:1]) * softplus
    lane = lax.broadcasted_iota(jnp.int32, (tb, LANE), 1)

    shift = int(math.log2(c))
    row = lax.broadcasted_iota(jnp.int32, (tb, tb), 0)
    col = lax.broadcasted_iota(jnp.int32, (tb, tb), 1)
    same = jnp.right_shift(row, shift) == jnp.right_shift(col, shift)
    incl = jnp.logical_and(same, col <= row)
    strict = jnp.logical_and(same, col < row)
    lincl = jnp.where(incl, 1.0, 0.0).astype(BF16)
    eye = jnp.where(row == col, 1.0, 0.0)

    hs = range(hb)
    q = [conv(0, h, wq_ref) for h in hs]
    k = [conv(1, h, wk_ref) for h in hs]
    v = [conv(2, h, wv_ref) for h in hs]
    q = [x * (lax.rsqrt(jnp.sum(x * x, axis=-1, keepdims=True) + EPS) * (dk ** -0.5)) for x in q]
    k = [x * lax.rsqrt(jnp.sum(x * x, axis=-1, keepdims=True) + EPS) for x in k]
    beta = [jnp.sum(jnp.where(lane == hg * hb + h, beta_all, 0.0), axis=-1, keepdims=True) for h in hs]
    g = [jnp.sum(jnp.where(lane == hg * hb + h + GDN_HEADS, g_all, 0.0), axis=-1, keepdims=True) for h in hs]

    g_hi = [x.astype(BF16).astype(F32) for x in g]
    g_lo = [(x - y).astype(BF16).astype(F32) for x, y in zip(g, g_hi)]
    dmat = [_dot(lincl, jnp.where(strict, x, 0.0).astype(BF16)) + _dot(lincl, jnp.where(strict, y, 0.0).astype(BF16))
            for x, y in zip(g_hi, g_lo)]
    gcb = [_dot(lincl, jnp.broadcast_to(x, (tb, dk)).astype(BF16))
           + _dot(lincl, jnp.broadcast_to(y, (tb, dk)).astype(BF16)) for x, y in zip(g_hi, g_lo)]
    decay = [jnp.where(incl, jnp.exp(x), 0.0) for x in dmat]

    kb = [x * y for x, y in zip(k, beta)]
    k_b = [x.astype(BF16) for x in k]
    kk = [_nt(x.astype(BF16), y) for x, y in zip(kb, k_b)]
    p = [jnp.where(strict, -(x * y), 0.0) for x, y in zip(kk, decay)]
    tmat = [eye + x for x in p]
    for _ in range(shift - 1):
        pb = [x.astype(BF16) for x in p]
        p = [_dot(x, x) for x in pb]
        tmat = [x + _dot(x.astype(BF16), y.astype(BF16)) for x, y in zip(tmat, p)]
    t_b = [x.astype(BF16) for x in tmat]
    egc = [jnp.exp(x) for x in gcb]
    u = [_dot(x, (y * z).astype(BF16)) for x, y, z in zip(t_b, v, beta)]
    w_b = [_dot(x, (y * z).astype(BF16)).astype(BF16) for x, y, z in zip(t_b, kb, egc)]
    attn = [jnp.where(incl, _nt(x.astype(BF16), y) * z, 0.0).astype(BF16) for x, y, z in zip(q, k_b, decay)]
    qg = [(x * y).astype(BF16) for x, y in zip(q, egc)]

    s = [s_ref[h] for h in hs]
    vnews = [[] for _ in hs]
    inters = [[] for _ in hs]
    for ci in range(tb // c):
        rows = slice(ci * c, (ci + 1) * c)
        s_b = [x.astype(BF16) for x in s]
        v_new = [u[h][rows] - _dot(w_b[h][rows], s_b[h]) for h in hs]
        for h in hs:
            inters[h].append(_dot(qg[h][rows], s_b[h]))
            vnews[h].append(v_new[h])
        glast = [x[(ci + 1) * c - 1:(ci + 1) * c, :] for x in gcb]
        kdec = [k[h][rows] * jnp.exp(glast[h] - gcb[h][rows]) for h in hs]
        s = [s[h] * jnp.exp(glast[h]) + _tn(kdec[h].astype(BF16), v_new[h].astype(BF16)) for h in hs]
    for h in hs:
        s_ref[h] = s[h]
        sout_ref[0, h] = s[h]
    cat = lambda xs: xs[0] if len(xs) == 1 else jnp.concatenate(xs, axis=0)
    o = [cat(inters[h]) + _dot(attn[h], cat(vnews[h]).astype(BF16)) for h in hs]
    for h in hs:
        z = z_ref[:, h * dk:(h + 1) * dk].astype(F32)
        o_ref[:, h * dk:(h + 1) * dk] = (_rms(o[h], gn_ref[...]) * (z * _sigmoid(z))).astype(BF16)

    xbuf[0:8, :] = xbuf[tb:tb + 8, :]


def _gdn(main, small, conv0, ssm0, w_conv, gate_rows, gnorm_w, batch, seq, tb, c, hb):
    n_t = seq // tb
    hh = GDN_HEADS
    dk = GDN_DK
    wd = hb * dk

    def blk(off):
        return pl.BlockSpec((tb, wd), lambda b, h, t: (b * n_t + t, off // hb + h))

    def cblk(off):
        return pl.BlockSpec((1, CONV_W - 1, wd), lambda b, h, t: (b, 0, off // hb + h))

    def wblk(off):
        return pl.BlockSpec((CONV_W, wd), lambda b, h, t: (0, off // hb + h))

    return pl.pallas_call(
        functools.partial(_gdn_kernel, tb=tb, c=c, hb=hb),
        grid=(batch, hh // hb, n_t),
        in_specs=[
            blk(_QB), blk(_KB), blk(_VB), blk(_ZGB),
            pl.BlockSpec((tb, LANE), lambda b, h, t: (b * n_t + t, 1)),
            cblk(_QB), cblk(_KB), cblk(_VB),
            pl.BlockSpec((1, hb, dk, GDN_DV), lambda b, h, t: (b, h, 0, 0)),
            wblk(_QB), wblk(_KB), wblk(_VB),
            pl.BlockSpec((2, LANE), lambda b, h, t: (0, 0)),
            pl.BlockSpec((1, GDN_DV), lambda b, h, t: (0, 0)),
        ],
        out_specs=[
            pl.BlockSpec((tb, wd), lambda b, h, t: (b * n_t + t, h)),
            pl.BlockSpec((1, hb, dk, GDN_DV), lambda b, h, t: (b, h, 0, 0)),
        ],
        out_shape=[
            jax.ShapeDtypeStruct((batch * seq, hh * GDN_DV), BF16),
            jax.ShapeDtypeStruct((batch, hh, dk, GDN_DV), F32),
        ],
        scratch_shapes=[pltpu.VMEM((tb + 8, 3 * wd), F32), pltpu.VMEM((hb, dk, GDN_DV), F32)],
        compiler_params=_params("parallel", "parallel", "arbitrary"),
        name="gdn",
    )(main, main, main, main, small, conv0, conv0, conv0, ssm0, w_conv, w_conv, w_conv, gate_rows, gnorm_w)


def _mla_prep_kernel(cq_ref, ckv_ref, sm_ref, cosq_ref, sinq_ref, cosk_ref, sink_ref, qnw_ref, kvnw_ref,
                     wqt_ref, wuk_ref, wuvt_ref, qt_ref, k_ref, vt_ref, lat_ref, kr_ref):
    scale = QK_DIM ** -0.5 * math.log2(math.e)
    half = ROPE_DIM // 2
    tm = cq_ref.shape[0]
    cqn = _rms(cq_ref[...].astype(F32), qnw_ref[...]).astype(BF16)
    ckvn32 = _rms(ckv_ref[...].astype(F32), kvnw_ref[...])
    lat_ref[0] = ckvn32
    ckvn = ckvn32.astype(BF16)
    sm = sm_ref[...]
    krope32 = sm[:, 0:ROPE_DIM] * cosk_ref[...] + sm[:, ROPE_DIM:2 * ROPE_DIM] * sink_ref[...]
    kr_ref[0] = krope32
    krope = krope32.astype(BF16)
    knope = _dot(ckvn, wuk_ref[...])
    cq_cos = cosq_ref[...]
    cq_sin = sinq_ref[...]
    for h in range(MLA_HEADS):
        k_ref[0, h, :, 0:NOPE_DIM] = knope[:, h * NOPE_DIM:(h + 1) * NOPE_DIM].astype(BF16)
        k_ref[0, h, :, NOPE_DIM:QK_DIM] = krope
        vt_ref[0, h, 0:V_DIM, :] = _nt(wuvt_ref[h], ckvn).astype(BF16)
        vt_ref[0, h, V_DIM:V_ROWS, :] = jnp.ones((V_ROWS - V_DIM, tm), BF16)
        qt = _nt(wqt_ref[h], cqn)
        x1 = qt[NOPE_DIM:NOPE_DIM + half]
        x2 = qt[NOPE_DIM + half:QK_DIM]
        qt_ref[0, h, 0:NOPE_DIM, :] = (qt[0:NOPE_DIM] * scale).astype(BF16)
        qt_ref[0, h, NOPE_DIM:NOPE_DIM + half, :] = ((x1 * cq_cos - x2 * cq_sin) * scale).astype(BF16)
        qt_ref[0, h, NOPE_DIM + half:QK_DIM, :] = ((x2 * cq_cos + x1 * cq_sin) * scale).astype(BF16)


def _mla_prep(main, small, cos_t, sin_t, cos2, sin2, qn_w, kvn_w, wq_t, w_uk2, wuv_t, batch, seq, tm):
    n_t = seq // tm
    hh = MLA_HEADS
    half = ROPE_DIM // 2
    const = lambda *shape: pl.BlockSpec(shape, lambda b, i: (0,) * len(shape))
    return pl.pallas_call(
        _mla_prep_kernel,
        grid=(batch, n_t),
        in_specs=[
            pl.BlockSpec((tm, Q_LORA), lambda b, i: (b * n_t + i, _CQ_BLK)),
            pl.BlockSpec((tm, KV_LORA), lambda b, i: (b * n_t + i, _CKV_BLK)),
            pl.BlockSpec((tm, LANE), lambda b, i: (b * n_t + i, 0)),
            pl.BlockSpec((half, tm), lambda b, i: (0, i)),
            pl.BlockSpec((half, tm), lambda b, i: (0, i)),
            pl.BlockSpec((tm, ROPE_DIM), lambda b, i: (i, 0)),
            pl.BlockSpec((tm, ROPE_DIM), lambda b, i: (i, 0)),
            const(1, Q_LORA), const(1, KV_LORA),
            const(hh, QK_DIM, Q_LORA), const(KV_LORA, hh * NOPE_DIM), const(hh, V_DIM, KV_LORA),
        ],
        out_specs=[
            pl.BlockSpec((1, hh, QK_DIM, tm), lambda b, i: (b, 0, 0, i)),
            pl.BlockSpec((1, hh, tm, QK_DIM), lambda b, i: (b, 0, i, 0)),
            pl.BlockSpec((1, hh, V_ROWS, tm), lambda b, i: (b, 0, 0, i)),
            pl.BlockSpec((1, tm, KV_LORA), lambda b, i: (b, i, 0)),
            pl.BlockSpec((1, tm, ROPE_DIM), lambda b, i: (b, i, 0)),
        ],
        out_shape=[
            jax.ShapeDtypeStruct((batch, hh, QK_DIM, seq), BF16),
            jax.ShapeDtypeStruct((batch, hh, seq, QK_DIM), BF16),
            jax.ShapeDtypeStruct((batch, hh, V_ROWS, seq), BF16),
            jax.ShapeDtypeStruct((batch, seq, KV_LORA), F32),
            jax.ShapeDtypeStruct((batch, seq, ROPE_DIM), F32),
        ],
        compiler_params=_params("parallel", "parallel"),
        name="mla_prep",
    )(main, main, small, cos_t, sin_t, cos2, sin2, qn_w, kvn_w, wq_t, w_uk2, wuv_t)


_NEG = -1e30


def _flash_kernel(qt_ref, k_ref, vt_ref, z_ref, bias_ref, o_ref, acc_ref, m_ref, s_buf, p_buf, mb_buf, al_buf,
                  *, tq, tk, unroll):
    r = tq // tk
    i = pl.program_id(2)
    nb = r * (i + 1)

    m_ref[...] = jnp.full(m_ref.shape, _NEG, F32)
    acc_ref[...] = jnp.zeros(acc_ref.shape, F32)
    p_buf[...] = jnp.zeros(p_buf.shape, BF16)
    al_buf[...] = jnp.ones(al_buf.shape, F32)
    s_buf[1] = jnp.full(s_buf.shape[1:], _NEG, F32)
    mb_buf[1] = jnp.full(mb_buf.shape[1:], _NEG, F32)

    def cols_of(u):
        return slice(max(u, 0) * tk, tq)

    def stage_a(blk, slot, u):
        cols = cols_of(u)
        off = pl.multiple_of(blk * tk, tk)
        s = _dot(k_ref[0, 0, pl.ds(off, tk), :], qt_ref[0, 0, :, cols])
        if u >= 0:
            s = s + bias_ref[u, :, cols]
        s_buf[slot, :, cols] = s
        mb_buf[slot, :, cols] = jnp.max(s, axis=0, keepdims=True)

    def stage_b(slot, u):
        cols = cols_of(u)
        m_prev = m_ref[:, cols]
        m_new = jnp.maximum(m_prev, mb_buf[slot, :, cols])
        al_buf[slot, :, cols] = jnp.exp2(m_prev - m_new)
        p_buf[slot, :, cols] = jnp.exp2(s_buf[slot, :, cols] - m_new).astype(BF16)
        m_ref[:, cols] = m_new

    def stage_c(blk, slot, u):
        cols = cols_of(u)
        off = pl.multiple_of(jnp.maximum(blk, 0) * tk, tk)
        acc_ref[:, cols] = (acc_ref[:, cols] * al_buf[slot, :, cols]
                            + _dot(vt_ref[0, 0, :, pl.ds(off, tk)], p_buf[slot, :, cols]))

    def pair(e, u):
        stage_c(e - 2, 0, u - 2)
        stage_a(e, 0, u)
        stage_b(1, u - 1)
        stage_c(e - 1, 1, u - 1)
        stage_a(e + 1, 1, u + 1 if u >= 0 else u)
        stage_b(0, u)

    def body(jp, carry):
        for x in range(unroll):
            pair(2 * (unroll * jp + x), -4)
        return carry

    lax.fori_loop(0, i * (r // 2) // unroll, body, 0)
    for x in range(r // 2):
        pair(nb - r + 2 * x, 2 * x)
    stage_c(nb - 2, 0, r - 2)
    stage_b(1, r - 1)
    stage_c(nb - 1, 1, r - 1)

    o = (acc_ref[0:V_DIM, :] / acc_ref[V_DIM:V_DIM + 1, :]).T
    z = z_ref[...].astype(F32)
    o_ref[...] = (o * (z * _sigmoid(z))).astype(BF16)


def _flash(q_t, k_all, v_t, main, batch, seq, tq, tk):
    n_q = seq // tq
    r = tq // tk
    unroll = r // 2
    assert r % 2 == 0 and tq % tk == 0
    hh = MLA_HEADS
    shift = int(math.log2(CHUNK))
    kc = jnp.right_shift(jnp.arange(tq, dtype=jnp.int32), shift).reshape(r, tk, 1)
    qc = jnp.right_shift(jnp.arange(tq, dtype=jnp.int32), shift).reshape(1, 1, tq)
    bias = jnp.where(kc <= qc, 0.0, _NEG).astype(F32)
    return pl.pallas_call(
        functools.partial(_flash_kernel, tq=tq, tk=tk, unroll=unroll),
        grid=(batch, hh, n_q),
        in_specs=[
            pl.BlockSpec((1, 1, QK_DIM, tq), lambda b, h, i: (b, h, 0, i)),
            pl.BlockSpec((1, 1, seq, QK_DIM), lambda b, h, i: (b, h, 0, 0)),
            pl.BlockSpec((1, 1, V_ROWS, seq), lambda b, h, i: (b, h, 0, 0)),
            pl.BlockSpec((tq, V_DIM), lambda b, h, i: (b * n_q + i, _ZMB + h)),
            pl.BlockSpec((r, tk, tq), lambda b, h, i: (0, 0, 0)),
        ],
        out_specs=pl.BlockSpec((tq, V_DIM), lambda b, h, i: (b * n_q + i, h)),
        out_shape=jax.ShapeDtypeStruct((batch * seq, hh * V_DIM), BF16),
        scratch_shapes=[pltpu.VMEM((V_ROWS, tq), F32), pltpu.VMEM((1, tq), F32),
                        pltpu.VMEM((2, tk, tq), F32), pltpu.VMEM((2, tk, tq), BF16),
                        pltpu.VMEM((2, 1, tq), F32), pltpu.VMEM((2, 1, tq), F32)],
        compiler_params=_params("parallel", "parallel", "arbitrary"),
        name="flash",
    )(q_t, k_all, v_t, main, bias)


def _mla_decode_kernel(cq_ref, ckv_ref, sm_ref, z_ref, plat_ref, pkr_ref, cos_ref, sin_ref, qnw_ref, kvnw_ref,
                       wqn_ref, wqr_ref, wqrr_ref, wukt_ref, wuv_ref, o_ref, lat_ref, kr_ref, *, past, t):
    scale = QK_DIM ** -0.5
    cqn = _rms(cq_ref[...].astype(F32), qnw_ref[...]).astype(BF16)
    ckvn32 = _rms(ckv_ref[...].astype(F32), kvnw_ref[...])
    lat_ref[0] = ckvn32
    ckvn = ckvn32.astype(BF16)
    sm = sm_ref[...]
    cos2 = cos_ref[...]
    sin2 = sin_ref[...]
    krope32 = sm[:, 0:ROPE_DIM] * cos2 + sm[:, ROPE_DIM:2 * ROPE_DIM] * sin2
    kr_ref[0] = krope32
    krope = krope32.astype(BF16)

    qlats, qropes = [], []
    for h in range(MLA_HEADS):
        qn = _dot(cqn, wqn_ref[h]).astype(BF16)
        qlats.append((_dot(qn, wukt_ref[h]) * scale).astype(BF16))
        qr = _dot(cqn, wqr_ref[h]) * cos2 + _dot(cqn, wqrr_ref[h]) * sin2
        qropes.append((qr * scale).astype(BF16))
    qlat = jnp.concatenate(qlats, axis=0)
    qrope = jnp.concatenate(qropes, axis=0)

    plat = plat_ref[0].astype(BF16)
    pkr = pkr_ref[0].astype(BF16)
    s_past = _nt(qlat, plat) + _nt(qrope, pkr)
    s_new = _nt(qlat, ckvn) + _nt(qrope, krope)
    rows = MLA_HEADS * t
    qpos = past + jnp.bitwise_and(lax.broadcasted_iota(jnp.int32, (rows, t), 0), t - 1)
    kpos = past + lax.broadcasted_iota(jnp.int32, (rows, t), 1)
    shift = int(math.log2(CHUNK))
    visible = kpos < jnp.left_shift(jnp.right_shift(qpos, shift) + 1, shift)
    s_new = jnp.where(visible, s_new, -jnp.inf)
    m = jnp.maximum(jnp.max(s_past, axis=-1, keepdims=True), jnp.max(s_new, axis=-1, keepdims=True))
    p_past = jnp.exp(s_past - m)
    p_new = jnp.exp(s_new - m)
    l = jnp.sum(p_past, axis=-1, keepdims=True) + jnp.sum(p_new, axis=-1, keepdims=True)
    o_lat = (_dot(p_past.astype(BF16), plat) + _dot(p_new.astype(BF16), ckvn)) / l
    z = z_ref[...].astype(F32)
    gate = z * _sigmoid(z)
    for h in range(MLA_HEADS):
        o_h = _dot(o_lat[h * t:(h + 1) * t].astype(BF16), wuv_ref[h])
        cols = slice(h * V_DIM, (h + 1) * V_DIM)
        o_ref[:, cols] = (o_h * gate[:, cols]).astype(BF16)


def _mla_decode(main, small, past_lat, past_kr, cos2, sin2, qn_w, kvn_w, wq_n, wq_r, wq_rr, wuk_t, wuv, batch, t):
    past = past_lat.shape[1]
    hh = MLA_HEADS
    const = lambda *shape: pl.BlockSpec(shape, lambda b: (0,) * len(shape))
    return pl.pallas_call(
        functools.partial(_mla_decode_kernel, past=past, t=t),
        grid=(batch,),
        in_specs=[
            pl.BlockSpec((t, Q_LORA), lambda b: (b, _CQ_BLK)),
            pl.BlockSpec((t, KV_LORA), lambda b: (b, _CKV_BLK)),
            pl.BlockSpec((t, LANE), lambda b: (b, 0)),
            pl.BlockSpec((t, hh * V_DIM), lambda b: (b, _ZMB // hh)),
            pl.BlockSpec((1, past, KV_LORA), lambda b: (b, 0, 0)),
            pl.BlockSpec((1, past, ROPE_DIM), lambda b: (b, 0, 0)),
            const(t, ROPE_DIM), const(t, ROPE_DIM),
            const(1, Q_LORA), const(1, KV_LORA),
            const(hh, Q_LORA, NOPE_DIM), const(hh, Q_LORA, ROPE_DIM), const(hh, Q_LORA, ROPE_DIM),
            const(hh, NOPE_DIM, KV_LORA), const(hh, KV_LORA, V_DIM),
        ],
        out_specs=[
            pl.BlockSpec((t, hh * V_DIM), lambda b: (b, 0)),
            pl.BlockSpec((1, t, KV_LORA), lambda b: (b, 0, 0)),
            pl.BlockSpec((1, t, ROPE_DIM), lambda b: (b, 0, 0)),
        ],
        out_shape=[
            jax.ShapeDtypeStruct((batch * t, hh * V_DIM), BF16),
            jax.ShapeDtypeStruct((batch, t, KV_LORA), F32),
            jax.ShapeDtypeStruct((batch, t, ROPE_DIM), F32),
        ],
        compiler_params=_params("parallel"),
        name="mla_decode",
    )(main, main, small, main, past_lat, past_kr, cos2, sin2, qn_w, kvn_w, wq_n, wq_r, wq_rr, wuk_t, wuv)


def _rope_tables(p0, t):
    pos = jnp.arange(p0, p0 + t, dtype=jnp.int32)
    inv = ROPE_THETA ** (-jnp.arange(0, ROPE_DIM, 2, dtype=F32) / ROPE_DIM)
    ang = pos.astype(F32)[:, None] * inv[None, :]
    return jnp.cos(ang), jnp.sin(ang)


def _pick(n, prefs):
    for p in prefs:
        if n % p == 0:
            return p
    return n


def _layer(x, conv0, ssm0, past_lat, past_kr, lw, lnf_w):
    (ln_w, w_in, w_conv, a_log, dt_bias, gn_w, qn_w, w_q_up, kvn_w, w_uk, w_uv, w_out) = lw
    batch, t, d = x.shape
    n = batch * t
    hh = MLA_HEADS
    half = ROPE_DIM // 2
    x2d = x.reshape(n, d)

    sizes = (3 * GDN_HEADS * GDN_DK, GDN_HEADS * GDN_DV, GDN_HEADS, GDN_HEADS, Q_LORA, KV_LORA, ROPE_DIM, hh * V_DIM)
    offs = [0]
    for s in sizes:
        offs.append(offs[-1] + s)
    w_qkv, w_zg, w_bg, w_ag, w_cq, w_ckv, w_kr, w_zm = (w_in[:, offs[i]:offs[i + 1]] for i in range(8))
    w_main = jnp.concatenate([w_qkv, w_zg, w_zm, w_cq, w_ckv], axis=1).astype(BF16)
    w_small = jnp.concatenate(
        [w_kr, w_kr[:, half:], w_kr[:, :half], w_bg, w_ag,
         jnp.zeros((d, _SMALL_COLS - 2 * ROPE_DIM - 2 * GDN_HEADS), F32)], axis=1).astype(BF16)
    pad = jnp.zeros((LANE - 2 * GDN_HEADS,), F32)
    gate_rows = jnp.stack([jnp.concatenate([jnp.zeros((GDN_HEADS,), F32), a_log, pad]),
                           jnp.concatenate([jnp.zeros((GDN_HEADS,), F32), dt_bias, pad])])

    tm = _pick(n, (1024, 512, 256))
    main, small = _in_proj(x2d, ln_w.reshape(1, d), w_main, w_small, tm, 1024)

    c = min(GDN_CHUNK, t)
    tb = _pick(t, (256, 128, 64))
    o_g, new_ssm = _gdn(main, small, conv0, ssm0, w_conv, gate_rows, gn_w.reshape(1, GDN_DV), batch, t, tb, c, 8)
    qkv_rows = main.reshape(batch, t, -1)[:, max(t - (CONV_W - 1), 0):, :3 * GDN_HEADS * GDN_DK].astype(F32)
    new_conv = jnp.concatenate([conv0, qkv_rows], axis=1)[:, -(CONV_W - 1):]

    p_len = past_lat.shape[1]
    cos, sin = _rope_tables(p_len, t)
    cos2 = jnp.concatenate([cos, cos], axis=1)
    sin2 = jnp.concatenate([-sin, sin], axis=1)
    qn_w2 = qn_w.reshape(1, Q_LORA)
    kvn_w2 = kvn_w.reshape(1, KV_LORA)
    if p_len == 0:
        wq_t = jnp.transpose(w_q_up, (1, 2, 0)).astype(BF16)
        w_uk2 = w_uk.reshape(KV_LORA, hh * NOPE_DIM).astype(BF16)
        wuv_t = jnp.transpose(w_uv, (1, 2, 0)).astype(BF16)
        tmq = _pick(t, (512, 256, 128))
        q_t, k_all, v_t, lat, kr = _mla_prep(main, small, cos.T, sin.T, cos2, sin2, qn_w2, kvn_w2,
                                             wq_t, w_uk2, wuv_t, batch, t, tmq)
        tq = _pick(t, (1024, 512, 256))
        o_m = _flash(q_t, k_all, v_t, main, batch, t, tq, tq // 4)
    else:
        wq_n = jnp.transpose(w_q_up[:, :, :NOPE_DIM], (1, 0, 2)).astype(BF16)
        wq_r = jnp.transpose(w_q_up[:, :, NOPE_DIM:], (1, 0, 2))
        wq_rr = jnp.concatenate([wq_r[..., half:], wq_r[..., :half]], axis=-1).astype(BF16)
        wuk_t = jnp.transpose(w_uk, (1, 2, 0)).astype(BF16)
        wuv = jnp.transpose(w_uv, (1, 0, 2)).astype(BF16)
        o_m, lat, kr = _mla_decode(main, small, past_lat, past_kr, cos2, sin2, qn_w2, kvn_w2,
                                   wq_n, wq_r.astype(BF16), wq_rr, wuk_t, wuv, batch, t)

    w_g = w_out[:GDN_HEADS * GDN_DV].astype(BF16)
    w_m = w_out[GDN_HEADS * GDN_DV:].astype(BF16)
    y = _out_proj(x2d, o_g, o_m, w_g, w_m, lnf_w.reshape(1, d), _pick(n, (512, 256)))
    return y.reshape(batch, t, d), new_conv, new_ssm, lat, kr


def kernel(x_prompt, x_sample, cache_mla_latent, cache_mla_krope, state_gdn_conv, state_gdn_ssm, ln_in_w, w_in, w_conv, a_log, dt_bias, gdn_norm_w, q_norm_w, w_q_up, kv_norm_w, w_uk, w_uv, w_out, ln_final_w):
    depth = w_in.shape[0]
    assert depth == 1, "the final norm is fused into the out-projection of the single layer"
    bp = x_prompt.shape[0]
    lw = (ln_in_w[0], w_in[0], w_conv[0], a_log[0], dt_bias[0], gdn_norm_w[0],
          q_norm_w[0], w_q_up[0], kv_norm_w[0], w_uk[0], w_uv[0], w_out[0])
    zc = jnp.zeros((bp, CONV_W - 1, 3 * GDN_HEADS * GDN_DK), F32)
    zs = jnp.zeros((bp, GDN_HEADS, GDN_DK, GDN_DV), F32)
    yp, c1, s1, l1, r1 = _layer(x_prompt, zc, zs, jnp.zeros((bp, 0, KV_LORA), F32),
                                jnp.zeros((bp, 0, ROPE_DIM), F32), lw, ln_final_w)
    ys, c2, s2, l2, r2 = _layer(x_sample, state_gdn_conv[0], state_gdn_ssm[0], cache_mla_latent[0],
                                cache_mla_krope[0], lw, ln_final_w)
    return (yp, ys, c1[None], s1[None], l1[None], r1[None], c2[None], s2[None], l2[None], r2[None])
```

```python
import functools
import math

import jax
import jax.numpy as jnp
from jax import lax
from jax.experimental import pallas as pl
from jax.experimental.pallas import tpu as pltpu

F32 = jnp.float32
BF16 = jnp.bfloat16

EPS = 1e-6
CHUNK = 64
GDN_CHUNK = 64
GDN_HEADS = 8
GDN_DK = 128
GDN_DV = 128
CONV_W = 4
MLA_HEADS = 8
Q_LORA = 512
KV_LORA = 512
NOPE_DIM = 128
ROPE_DIM = 64
V_DIM = 128
ROPE_THETA = 10000.0
QK_DIM = NOPE_DIM + ROPE_DIM
V_ROWS = V_DIM + 16

LANE = 128
MXU_COLS = 256
VMEM_LIMIT = 56 * 1024 * 1024

_QB, _KB, _VB, _ZGB, _ZMB = 0, 8, 16, 24, 32
_CQ_BLK, _CKV_BLK = 10, 11
_MAIN_COLS = 6144
_SMALL_COLS = 256


def _nt(a, b):
    return lax.dot_general(a, b, (((1,), (1,)), ((), ())), preferred_element_type=F32)


def _tn(a, b):
    return lax.dot_general(a, b, (((0,), (0,)), ((), ())), preferred_element_type=F32)


def _dot(a, b):
    return jnp.dot(a, b, preferred_element_type=F32)


def _rms(x, w):
    return x * lax.rsqrt(jnp.mean(x * x, axis=-1, keepdims=True) + EPS) * w


def _sigmoid(x):
    return 1.0 / (1.0 + jnp.exp(-x))


def _params(*sem):
    return pltpu.CompilerParams(dimension_semantics=sem, vmem_limit_bytes=VMEM_LIMIT)


def _in_proj_kernel(x_ref, lnw_ref, wm_ref, ws_ref, main_ref, small_ref, h_ref, *, rows):
    @pl.when(pl.program_id(1) == 0)
    def _():
        def body(r, c):
            sl = pl.ds(pl.multiple_of(r * rows, rows), rows)
            h_ref[sl, :] = _rms(x_ref[sl, :], lnw_ref[...]).astype(BF16)
            return c
        lax.fori_loop(0, x_ref.shape[0] // rows, body, 0)
        small_ref[...] = _dot(h_ref[...], ws_ref[...])

    main_ref[...] = _dot(h_ref[...], wm_ref[...]).astype(BF16)


def _in_proj(x2d, ln_w, w_main, w_small, tm, tn):
    n, d = x2d.shape
    nm, ns = w_main.shape[1], w_small.shape[1]
    rows = min(tm, 128)
    return pl.pallas_call(
        functools.partial(_in_proj_kernel, rows=rows),
        grid=(n // tm, nm // tn),
        in_specs=[
            pl.BlockSpec((tm, d), lambda i, j: (i, 0)),
            pl.BlockSpec((1, d), lambda i, j: (0, 0)),
            pl.BlockSpec((d, tn), lambda i, j: (0, j)),
            pl.BlockSpec((d, ns), lambda i, j: (0, 0)),
        ],
        out_specs=[
            pl.BlockSpec((tm, tn), lambda i, j: (i, j)),
            pl.BlockSpec((tm, ns), lambda i, j: (i, 0)),
        ],
        out_shape=[
            jax.ShapeDtypeStruct((n, nm), BF16),
            jax.ShapeDtypeStruct((n, ns), F32),
        ],
        scratch_shapes=[pltpu.VMEM((tm, d), BF16)],
        compiler_params=_params("parallel", "arbitrary"),
        name="in_proj",
    )(x2d, ln_w, w_main, w_small)


def _out_proj_kernel(x_ref, og_ref, om_ref, wg_ref, wm_ref, lnf_ref, y_ref):
    acc = _dot(og_ref[...], wg_ref[...]) + _dot(om_ref[...], wm_ref[...])
    y_ref[...] = _rms(x_ref[...] + acc, lnf_ref[...])


def _out_proj(x2d, o_g, o_m, w_g, w_m, lnf_w, tm):
    n, d = x2d.shape
    kg, km = o_g.shape[1], o_m.shape[1]
    return pl.pallas_call(
        _out_proj_kernel,
        grid=(n // tm,),
        in_specs=[
            pl.BlockSpec((tm, d), lambda i: (i, 0)),
            pl.BlockSpec((tm, kg), lambda i: (i, 0)),
            pl.BlockSpec((tm, km), lambda i: (i, 0)),
            pl.BlockSpec((kg, d), lambda i: (0, 0)),
            pl.BlockSpec((km, d), lambda i: (0, 0)),
            pl.BlockSpec((1, d), lambda i: (0, 0)),
        ],
        out_specs=pl.BlockSpec((tm, d), lambda i: (i, 0)),
        out_shape=jax.ShapeDtypeStruct((n, d), F32),
        compiler_params=_params("parallel"),
        name="out_proj",
    )(x2d, o_g, o_m, w_g, w_m, lnf_w)


def _gdn_kernel(q_ref, k_ref, v_ref, z_ref, sm_ref, cq0_ref, ck0_ref, cv0_ref, s0_ref,
                wq_ref, wk_ref, wv_ref, gp_ref, gn_ref, o_ref, sout_ref, xbuf, s_ref, *, tb, c, hb):
    hg = pl.program_id(1)
    t = pl.program_id(2)
    dk = GDN_DK
    wd = hb * dk

    @pl.when(t == 0)
    def _():
        xbuf[0:8, :] = jnp.zeros((8, 3 * wd), F32)
        xbuf[8 - (CONV_W - 1):8, 0:wd] = cq0_ref[0]
        xbuf[8 - (CONV_W - 1):8, wd:2 * wd] = ck0_ref[0]
        xbuf[8 - (CONV_W - 1):8, 2 * wd:3 * wd] = cv0_ref[0]
        s_ref[...] = s0_ref[0]

    xbuf[8:8 + tb, 0:wd] = q_ref[...].astype(F32)
    xbuf[8:8 + tb, wd:2 * wd] = k_ref[...].astype(F32)
    xbuf[8:8 + tb, 2 * wd:3 * wd] = v_ref[...].astype(F32)

    def conv(idx, hh, w_ref):
        w = w_ref[:, hh * dk:(hh + 1) * dk]
        cols = slice(idx * wd + hh * dk, idx * wd + (hh + 1) * dk)
        y = xbuf[8:8 + tb, cols] * w[CONV_W - 1:CONV_W]
        for i in range(1, CONV_W):
            y = y + xbuf[8 - i:8 - i + tb, cols] * w[CONV_W - 1 - i:CONV_W - i]
        return y * _sigmoid(y)

    sm = sm_ref[...]
    gp = gp_ref[...]
    beta_all = _sigmoid(sm)
    xg = sm + gp[1:2]
    softplus = jnp.maximum(xg, 0.0) + jnp.log(1.0 + jnp.exp(-jnp.abs(xg)))
    g_all = -jnp.exp(gp[0:1]) * softplus
    lane = lax.broadcasted_iota(jnp.int32, (tb, LANE), 1)

    shift = int(math.log2(c))
    nc = tb // c
    row = lax.broadcasted_iota(jnp.int32, (tb, tb), 0)
    col = lax.broadcasted_iota(jnp.int32, (tb, tb), 1)
    same = jnp.right_shift(row, shift) == jnp.right_shift(col, shift)
    lincl = jnp.where(jnp.logical_and(same, col <= row), 1.0, 0.0).astype(BF16)
    bd_mask = jnp.where(same, 1.0, 0.0).astype(BF16)
    rl = lax.broadcasted_iota(jnp.int32, (c, tb), 0)
    ll = lax.broadcasted_iota(jnp.int32, (c, tb), 1)
    jl = jnp.bitwise_and(ll, c - 1)
    lb = jnp.right_shift(ll, shift)
    incl = jl <= rl
    strict = jl < rl
    eye = jnp.where(jl == rl, 1.0, 0.0)

    def to_ls(x):
        out = x[(nc - 1) * c:nc * c]
        for n in range(nc - 2, -1, -1):
            out = jnp.where(lb == n, x[n * c:(n + 1) * c], out)
        return out

    def to_bd(x_b):
        if nc == 1:
            return x_b
        return jnp.concatenate([x_b] * nc, axis=0) * bd_mask

    def widen(x):
        return x[:, :tb] if tb <= dk else jnp.concatenate([x] * (tb // dk), axis=1)

    hs = range(hb)
    q = [conv(0, h, wq_ref) for h in hs]
    k = [conv(1, h, wk_ref) for h in hs]
    v = [conv(2, h, wv_ref) for h in hs]
    q = [x * (lax.rsqrt(jnp.sum(x * x, axis=-1, keepdims=True) + EPS) * (dk ** -0.5)) for x in q]
    k = [x * lax.rsqrt(jnp.sum(x * x, axis=-1, keepdims=True) + EPS) for x in k]
    beta = [jnp.sum(jnp.where(lane == hg * hb + h, beta_all, 0.0), axis=-1, keepdims=True) for h in hs]
    g = [jnp.sum(jnp.where(lane == hg * hb + h + GDN_HEADS, g_all, 0.0), axis=-1, keepdims=True) for h in hs]

    g_hi = [x.astype(BF16).astype(F32) for x in g]
    g_lo = [(x - y).astype(BF16).astype(F32) for x, y in zip(g, g_hi)]
    gsum = [_dot(lincl, jnp.concatenate([jnp.broadcast_to(x, (tb, dk)), jnp.broadcast_to(y, (tb, dk))],
                                        axis=1).astype(BF16)) for x, y in zip(g_hi, g_lo)]
    gcb = [x[:, :dk] + x[:, dk:] for x in gsum]
    gc_row = [x.T[0:1, :] for x in gcb]
    decay = [jnp.where(incl, jnp.exp(jnp.where(incl, to_ls(widen(x)) - y, 0.0)), 0.0)
             for x, y in zip(gcb, gc_row)]

    kb = [x * y for x, y in zip(k, beta)]
    k_b = [x.astype(BF16) for x in k]
    kk = [to_ls(_nt(x.astype(BF16), y)) for x, y in zip(kb, k_b)]
    p = [jnp.where(strict, -(x * y), 0.0) for x, y in zip(kk, decay)]
    tmat = [eye + x for x in p]
    p_bd = [to_bd(x.astype(BF16)) for x in p]
    for _ in range(shift - 1):
        p = [_dot(x.astype(BF16), y) for x, y in zip(p, p_bd)]
        p_bd = [to_bd(x.astype(BF16)) for x in p]
        tmat = [x + _dot(x.astype(BF16), y) for x, y in zip(tmat, p_bd)]
    t_bd = [to_bd(x.astype(BF16)) for x in tmat]
    egc = [jnp.exp(x) for x in gcb]
    uw = [_dot(x, jnp.concatenate([y * z, w * e], axis=1).astype(BF16))
          for x, y, z, w, e in zip(t_bd, v, beta, kb, egc)]
    u = [x[:, :GDN_DV] for x in uw]
    w_b = [x[:, GDN_DV:].astype(BF16) for x in uw]
    attn = [to_bd(jnp.where(incl, to_ls(_nt(x.astype(BF16), y)) * z, 0.0).astype(BF16))
            for x, y, z in zip(q, k_b, decay)]
    qg = [(x * y).astype(BF16) for x, y in zip(q, egc)]

    s = [s_ref[h] for h in hs]
    vnews = [[] for _ in hs]
    inters = [[] for _ in hs]
    for ci in range(tb // c):
        rows = slice(ci * c, (ci + 1) * c)
        s_b = [x.astype(BF16) for x in s]
        v_new = [u[h][rows] - _dot(w_b[h][rows], s_b[h]) for h in hs]
        for h in hs:
            inters[h].append(_dot(qg[h][rows], s_b[h]))
            vnews[h].append(v_new[h])
        glast = [x[(ci + 1) * c - 1:(ci + 1) * c, :] for x in gcb]
        kdec = [k[h][rows] * jnp.exp(glast[h] - gcb[h][rows]) for h in hs]
        s = [s[h] * jnp.exp(glast[h]) + _tn(kdec[h].astype(BF16), v_new[h].astype(BF16)) for h in hs]
    for h in hs:
        s_ref[h] = s[h]
        sout_ref[0, h] = s[h]
    cat = lambda xs: xs[0] if len(xs) == 1 else jnp.concatenate(xs, axis=0)
    o = [cat(inters[h]) + _dot(attn[h], cat(vnews[h]).astype(BF16)) for h in hs]
    for h in hs:
        z = z_ref[:, h * dk:(h + 1) * dk].astype(F32)
        o_ref[:, h * dk:(h + 1) * dk] = (_rms(o[h], gn_ref[...]) * (z * _sigmoid(z))).astype(BF16)

    xbuf[0:8, :] = xbuf[tb:tb + 8, :]


def _gdn(main, small, conv0, ssm0, w_conv, gate_rows, gnorm_w, batch, seq, tb, c, hb):
    n_t = seq // tb
    hh = GDN_HEADS
    dk = GDN_DK
    wd = hb * dk

    def blk(off):
        return pl.BlockSpec((tb, wd), lambda b, h, t: (b * n_t + t, off // hb + h))

    def cblk(off):
        return pl.BlockSpec((1, CONV_W - 1, wd), lambda b, h, t: (b, 0, off // hb + h))

    def wblk(off):
        return pl.BlockSpec((CONV_W, wd), lambda b, h, t: (0, off // hb + h))

    return pl.pallas_call(
        functools.partial(_gdn_kernel, tb=tb, c=c, hb=hb),
        grid=(batch, hh // hb, n_t),
        in_specs=[
            blk(_QB), blk(_KB), blk(_VB), blk(_ZGB),
            pl.BlockSpec((tb, LANE), lambda b, h, t: (b * n_t + t, 1)),
            cblk(_QB), cblk(_KB), cblk(_VB),
            pl.BlockSpec((1, hb, dk, GDN_DV), lambda b, h, t: (b, h, 0, 0)),
            wblk(_QB), wblk(_KB), wblk(_VB),
            pl.BlockSpec((2, LANE), lambda b, h, t: (0, 0)),
            pl.BlockSpec((1, GDN_DV), lambda b, h, t: (0, 0)),
        ],
        out_specs=[
            pl.BlockSpec((tb, wd), lambda b, h, t: (b * n_t + t, h)),
            pl.BlockSpec((1, hb, dk, GDN_DV), lambda b, h, t: (b, h, 0, 0)),
        ],
        out_shape=[
            jax.ShapeDtypeStruct((batch * seq, hh * GDN_DV), BF16),
            jax.ShapeDtypeStruct((batch, hh, dk, GDN_DV), F32),
        ],
        scratch_shapes=[pltpu.VMEM((tb + 8, 3 * wd), F32), pltpu.VMEM((hb, dk, GDN_DV), F32)],
        compiler_params=_params("parallel", "parallel", "arbitrary"),
        name="gdn",
    )(main, main, main, main, small, conv0, conv0, conv0, ssm0, w_conv, w_conv, w_conv, gate_rows, gnorm_w)


def _mla_prep_kernel(cq_ref, ckv_ref, sm_ref, cosq_ref, sinq_ref, cosk_ref, sink_ref, qnw_ref, kvnw_ref,
                     wqt_ref, wuk_ref, wuvt_ref, qt_ref, k_ref, vt_ref, lat_ref, kr_ref):
    scale = QK_DIM ** -0.5 * math.log2(math.e)
    half = ROPE_DIM // 2
    tm = cq_ref.shape[0]
    cqn = _rms(cq_ref[...].astype(F32), qnw_ref[...]).astype(BF16)
    ckvn32 = _rms(ckv_ref[...].astype(F32), kvnw_ref[...])
    lat_ref[0] = ckvn32
    ckvn = ckvn32.astype(BF16)
    sm = sm_ref[...]
    krope32 = sm[:, 0:ROPE_DIM] * cosk_ref[...] + sm[:, ROPE_DIM:2 * ROPE_DIM] * sink_ref[...]
    kr_ref[0] = krope32
    krope = krope32.astype(BF16)
    knope = _dot(ckvn, wuk_ref[...])
    cq_cos = cosq_ref[...]
    cq_sin = sinq_ref[...]
    for h in range(MLA_HEADS):
        k_ref[0, h, :, 0:NOPE_DIM] = knope[:, h * NOPE_DIM:(h + 1) * NOPE_DIM].astype(BF16)
        k_ref[0, h, :, NOPE_DIM:QK_DIM] = krope
        vt_ref[0, h, 0:V_DIM, :] = _nt(wuvt_ref[h], ckvn).astype(BF16)
        vt_ref[0, h, V_DIM:V_ROWS, :] = jnp.ones((V_ROWS - V_DIM, tm), BF16)
        qt = _nt(wqt_ref[h], cqn)
        x1 = qt[NOPE_DIM:NOPE_DIM + half]
        x2 = qt[NOPE_DIM + half:QK_DIM]
        qt_ref[0, h, 0:NOPE_DIM, :] = (qt[0:NOPE_DIM] * scale).astype(BF16)
        qt_ref[0, h, NOPE_DIM:NOPE_DIM + half, :] = ((x1 * cq_cos - x2 * cq_sin) * scale).astype(BF16)
        qt_ref[0, h, NOPE_DIM + half:QK_DIM, :] = ((x2 * cq_cos + x1 * cq_sin) * scale).astype(BF16)


def _mla_prep(main, small, cos_t, sin_t, cos2, sin2, qn_w, kvn_w, wq_t, w_uk2, wuv_t, batch, seq, tm):
    n_t = seq // tm
    hh = MLA_HEADS
    half = ROPE_DIM // 2
    const = lambda *shape: pl.BlockSpec(shape, lambda b, i: (0,) * len(shape))
    return pl.pallas_call(
        _mla_prep_kernel,
        grid=(batch, n_t),
        in_specs=[
            pl.BlockSpec((tm, Q_LORA), lambda b, i: (b * n_t + i, _CQ_BLK)),
            pl.BlockSpec((tm, KV_LORA), lambda b, i: (b * n_t + i, _CKV_BLK)),
            pl.BlockSpec((tm, LANE), lambda b, i: (b * n_t + i, 0)),
            pl.BlockSpec((half, tm), lambda b, i: (0, i)),
            pl.BlockSpec((half, tm), lambda b, i: (0, i)),
            pl.BlockSpec((tm, ROPE_DIM), lambda b, i: (i, 0)),
            pl.BlockSpec((tm, ROPE_DIM), lambda b, i: (i, 0)),
            const(1, Q_LORA), const(1, KV_LORA),
            const(hh, QK_DIM, Q_LORA), const(KV_LORA, hh * NOPE_DIM), const(hh, V_DIM, KV_LORA),
        ],
        out_specs=[
            pl.BlockSpec((1, hh, QK_DIM, tm), lambda b, i: (b, 0, 0, i)),
            pl.BlockSpec((1, hh, tm, QK_DIM), lambda b, i: (b, 0, i, 0)),
            pl.BlockSpec((1, hh, V_ROWS, tm), lambda b, i: (b, 0, 0, i)),
            pl.BlockSpec((1, tm, KV_LORA), lambda b, i: (b, i, 0)),
            pl.BlockSpec((1, tm, ROPE_DIM), lambda b, i: (b, i, 0)),
        ],
        out_shape=[
            jax.ShapeDtypeStruct((batch, hh, QK_DIM, seq), BF16),
            jax.ShapeDtypeStruct((batch, hh, seq, QK_DIM), BF16),
            jax.ShapeDtypeStruct((batch, hh, V_ROWS, seq), BF16),
            jax.ShapeDtypeStruct((batch, seq, KV_LORA), F32),
            jax.ShapeDtypeStruct((batch, seq, ROPE_DIM), F32),
        ],
        compiler_params=_params("parallel", "parallel"),
        name="mla_prep",
    )(main, main, small, cos_t, sin_t, cos2, sin2, qn_w, kvn_w, wq_t, w_uk2, wuv_t)


_NEG = -1e30


def _flash_kernel(qt_ref, k_ref, vt_ref, z_ref, bias_ref, o_ref, acc_ref, m_ref, s_buf, p_buf, mb_buf, al_buf,
                  *, tq, tk, unroll):
    r = tq // tk
    strip = min(tk, MXU_COLS)
    i = pl.program_id(2)
    nb = r * (i + 1)

    m_ref[...] = jnp.full(m_ref.shape, _NEG, F32)
    acc_ref[...] = jnp.zeros(acc_ref.shape, F32)
    p_buf[...] = jnp.zeros(p_buf.shape, BF16)
    al_buf[...] = jnp.ones(al_buf.shape, F32)
    s_buf[1] = jnp.full(s_buf.shape[1:], _NEG, F32)
    mb_buf[1] = jnp.full(mb_buf.shape[1:], _NEG, F32)

    def stage_a(blk, slot, u, cols):
        off = pl.multiple_of(blk * tk, tk)
        s = _dot(k_ref[0, 0, pl.ds(off, tk), :], qt_ref[0, 0, :, cols])
        if u >= 0:
            s = s + bias_ref[u, :, cols]
        s_buf[slot, :, cols] = s
        mb_buf[slot, :, cols] = jnp.max(s, axis=0, keepdims=True)

    def stage_b(slot, cols):
        m_prev = m_ref[:, cols]
        m_new = jnp.maximum(m_prev, mb_buf[slot, :, cols])
        al_buf[slot, :, cols] = jnp.exp2(m_prev - m_new)
        p_buf[slot, :, cols] = jnp.exp2(s_buf[slot, :, cols] - m_new).astype(BF16)
        m_ref[:, cols] = m_new

    def stage_c(blk, slot, cols):
        off = pl.multiple_of(jnp.maximum(blk, 0) * tk, tk)
        acc_ref[:, cols] = (acc_ref[:, cols] * al_buf[slot, :, cols]
                            + _dot(vt_ref[0, 0, :, pl.ds(off, tk)], p_buf[slot, :, cols]))

    def half_step(c, a, b):
        for c0 in range(0, tq, strip):
            cols = slice(c0, c0 + strip)
            if c is not None and c0 >= max(c[2], 0) * tk:
                stage_c(c[0], c[1], cols)
            if a is not None and c0 >= max(a[2], 0) * tk:
                stage_a(a[0], a[1], a[2], cols)
            if b is not None and c0 >= max(b[1], 0) * tk:
                stage_b(b[0], cols)

    def pair(e, u):
        half_step((e - 2, 0, u - 2), (e, 0, u), (1, u - 1))
        half_step((e - 1, 1, u - 1), (e + 1, 1, u + 1 if u >= 0 else u), (0, u))

    def body(jp, carry):
        for x in range(unroll):
            pair(2 * (unroll * jp + x), -4)
        return carry

    lax.fori_loop(0, i * (r // 2) // unroll, body, 0)
    for x in range(r // 2):
        pair(nb - r + 2 * x, 2 * x)
    half_step((nb - 2, 0, r - 2), None, (1, r - 1))
    half_step((nb - 1, 1, r - 1), None, None)

    o = (acc_ref[0:V_DIM, :] / acc_ref[V_DIM:V_DIM + 1, :]).T
    z = z_ref[...].astype(F32)
    o_ref[...] = (o * (z * _sigmoid(z))).astype(BF16)


def _flash(q_t, k_all, v_t, main, batch, seq, tq, tk):
    n_q = seq // tq
    r = tq // tk
    unroll = r // 2
    assert r % 2 == 0 and tq % tk == 0
    hh = MLA_HEADS
    shift = int(math.log2(CHUNK))
    kc = jnp.right_shift(jnp.arange(tq, dtype=jnp.int32), shift).reshape(r, tk, 1)
    qc = jnp.right_shift(jnp.arange(tq, dtype=jnp.int32), shift).reshape(1, 1, tq)
    bias = jnp.where(kc <= qc, 0.0, _NEG).astype(F32)
    return pl.pallas_call(
        functools.partial(_flash_kernel, tq=tq, tk=tk, unroll=unroll),
        grid=(batch, hh, n_q),
        in_specs=[
            pl.BlockSpec((1, 1, QK_DIM, tq), lambda b, h, i: (b, h, 0, i)),
            pl.BlockSpec((1, 1, seq, QK_DIM), lambda b, h, i: (b, h, 0, 0)),
            pl.BlockSpec((1, 1, V_ROWS, seq), lambda b, h, i: (b, h, 0, 0)),
            pl.BlockSpec((tq, V_DIM), lambda b, h, i: (b * n_q + i, _ZMB + h)),
            pl.BlockSpec((r, tk, tq), lambda b, h, i: (0, 0, 0)),
        ],
        out_specs=pl.BlockSpec((tq, V_DIM), lambda b, h, i: (b * n_q + i, h)),
        out_shape=jax.ShapeDtypeStruct((batch * seq, hh * V_DIM), BF16),
        scratch_shapes=[pltpu.VMEM((V_ROWS, tq), F32), pltpu.VMEM((1, tq), F32),
                        pltpu.VMEM((2, tk, tq), F32), pltpu.VMEM((2, tk, tq), BF16),
                        pltpu.VMEM((2, 1, tq), F32), pltpu.VMEM((2, 1, tq), F32)],
        compiler_params=_params("parallel", "parallel", "arbitrary"),
        name="flash",
    )(q_t, k_all, v_t, main, bias)


def _mla_decode_kernel(cq_ref, ckv_ref, sm_ref, z_ref, plat_ref, pkr_ref, cos_ref, sin_ref, qnw_ref, kvnw_ref,
                       wqn_ref, wqr_ref, wqrr_ref, wukt_ref, wuv_ref, o_ref, lat_ref, kr_ref, *, past, t):
    scale = QK_DIM ** -0.5
    cqn = _rms(cq_ref[...].astype(F32), qnw_ref[...]).astype(BF16)
    ckvn32 = _rms(ckv_ref[...].astype(F32), kvnw_ref[...])
    lat_ref[0] = ckvn32
    ckvn = ckvn32.astype(BF16)
    sm = sm_ref[...]
    cos2 = cos_ref[...]
    sin2 = sin_ref[...]
    krope32 = sm[:, 0:ROPE_DIM] * cos2 + sm[:, ROPE_DIM:2 * ROPE_DIM] * sin2
    kr_ref[0] = krope32
    krope = krope32.astype(BF16)

    qlats, qropes = [], []
    for h in range(MLA_HEADS):
        qn = _dot(cqn, wqn_ref[h]).astype(BF16)
        qlats.append((_dot(qn, wukt_ref[h]) * scale).astype(BF16))
        qr = _dot(cqn, wqr_ref[h]) * cos2 + _dot(cqn, wqrr_ref[h]) * sin2
        qropes.append((qr * scale).astype(BF16))
    qlat = jnp.concatenate(qlats, axis=0)
    qrope = jnp.concatenate(qropes, axis=0)

    plat = plat_ref[0].astype(BF16)
    pkr = pkr_ref[0].astype(BF16)
    s_past = _nt(qlat, plat) + _nt(qrope, pkr)
    s_new = _nt(qlat, ckvn) + _nt(qrope, krope)
    rows = MLA_HEADS * t
    qpos = past + jnp.bitwise_and(lax.broadcasted_iota(jnp.int32, (rows, t), 0), t - 1)
    kpos = past + lax.broadcasted_iota(jnp.int32, (rows, t), 1)
    shift = int(math.log2(CHUNK))
    visible = kpos < jnp.left_shift(jnp.right_shift(qpos, shift) + 1, shift)
    s_new = jnp.where(visible, s_new, -jnp.inf)
    m = jnp.maximum(jnp.max(s_past, axis=-1, keepdims=True), jnp.max(s_new, axis=-1, keepdims=True))
    p_past = jnp.exp(s_past - m)
    p_new = jnp.exp(s_new - m)
    l = jnp.sum(p_past, axis=-1, keepdims=True) + jnp.sum(p_new, axis=-1, keepdims=True)
    o_lat = (_dot(p_past.astype(BF16), plat) + _dot(p_new.astype(BF16), ckvn)) / l
    z = z_ref[...].astype(F32)
    gate = z * _sigmoid(z)
    for h in range(MLA_HEADS):
        o_h = _dot(o_lat[h * t:(h + 1) * t].astype(BF16), wuv_ref[h])
        cols = slice(h * V_DIM, (h + 1) * V_DIM)
        o_ref[:, cols] = (o_h * gate[:, cols]).astype(BF16)


def _mla_decode(main, small, past_lat, past_kr, cos2, sin2, qn_w, kvn_w, wq_n, wq_r, wq_rr, wuk_t, wuv, batch, t):
    past = past_lat.shape[1]
    hh = MLA_HEADS
    const = lambda *shape: pl.BlockSpec(shape, lambda b: (0,) * len(shape))
    return pl.pallas_call(
        functools.partial(_mla_decode_kernel, past=past, t=t),
        grid=(batch,),
        in_specs=[
            pl.BlockSpec((t, Q_LORA), lambda b: (b, _CQ_BLK)),
            pl.BlockSpec((t, KV_LORA), lambda b: (b, _CKV_BLK)),
            pl.BlockSpec((t, LANE), lambda b: (b, 0)),
            pl.BlockSpec((t, hh * V_DIM), lambda b: (b, _ZMB // hh)),
            pl.BlockSpec((1, past, KV_LORA), lambda b: (b, 0, 0)),
            pl.BlockSpec((1, past, ROPE_DIM), lambda b: (b, 0, 0)),
            const(t, ROPE_DIM), const(t, ROPE_DIM),
            const(1, Q_LORA), const(1, KV_LORA),
            const(hh, Q_LORA, NOPE_DIM), const(hh, Q_LORA, ROPE_DIM), const(hh, Q_LORA, ROPE_DIM),
            const(hh, NOPE_DIM, KV_LORA), const(hh, KV_LORA, V_DIM),
        ],
        out_specs=[
            pl.BlockSpec((t, hh * V_DIM), lambda b: (b, 0)),
            pl.BlockSpec((1, t, KV_LORA), lambda b: (b, 0, 0)),
            pl.BlockSpec((1, t, ROPE_DIM), lambda b: (b, 0, 0)),
        ],
        out_shape=[
            jax.ShapeDtypeStruct((batch * t, hh * V_DIM), BF16),
            jax.ShapeDtypeStruct((batch, t, KV_LORA), F32),
            jax.ShapeDtypeStruct((batch, t, ROPE_DIM), F32),
        ],
        compiler_params=_params("parallel"),
        name="mla_decode",
    )(main, main, small, main, past_lat, past_kr, cos2, sin2, qn_w, kvn_w, wq_n, wq_r, wq_rr, wuk_t, wuv)


def _rope_tables(p0, t):
    pos = jnp.arange(p0, p0 + t, dtype=jnp.int32)
    inv = ROPE_THETA ** (-jnp.arange(0, ROPE_DIM, 2, dtype=F32) / ROPE_DIM)
    ang = pos.astype(F32)[:, None] * inv[None, :]
    return jnp.cos(ang), jnp.sin(ang)


def _pick(n, prefs):
    for p in prefs:
        if n % p == 0:
            return p
    return n


def _layer(x, conv0, ssm0, past_lat, past_kr, lw, lnf_w):
    (ln_w, w_in, w_conv, a_log, dt_bias, gn_w, qn_w, w_q_up, kvn_w, w_uk, w_uv, w_out) = lw
    batch, t, d = x.shape
    n = batch * t
    hh = MLA_HEADS
    half = ROPE_DIM // 2
    x2d = x.reshape(n, d)

    sizes = (3 * GDN_HEADS * GDN_DK, GDN_HEADS * GDN_DV, GDN_HEADS, GDN_HEADS, Q_LORA, KV_LORA, ROPE_DIM, hh * V_DIM)
    offs = [0]
    for s in sizes:
        offs.append(offs[-1] + s)
    w_qkv, w_zg, w_bg, w_ag, w_cq, w_ckv, w_kr, w_zm = (w_in[:, offs[i]:offs[i + 1]] for i in range(8))
    w_main = jnp.concatenate([w_qkv, w_zg, w_zm, w_cq, w_ckv], axis=1).astype(BF16)
    w_small = jnp.concatenate(
        [w_kr, w_kr[:, half:], w_kr[:, :half], w_bg, w_ag,
         jnp.zeros((d, _SMALL_COLS - 2 * ROPE_DIM - 2 * GDN_HEADS), F32)], axis=1).astype(BF16)
    pad = jnp.zeros((LANE - 2 * GDN_HEADS,), F32)
    gate_rows = jnp.stack([jnp.concatenate([jnp.zeros((GDN_HEADS,), F32), a_log, pad]),
                           jnp.concatenate([jnp.zeros((GDN_HEADS,), F32), dt_bias, pad])])

    tm = _pick(n, (1024, 512, 256))
    main, small = _in_proj(x2d, ln_w.reshape(1, d), w_main, w_small, tm, 1024)

    c = min(GDN_CHUNK, t)
    tb = _pick(t, (256, 128, 64))
    o_g, new_ssm = _gdn(main, small, conv0, ssm0, w_conv, gate_rows, gn_w.reshape(1, GDN_DV), batch, t, tb, c, 8)
    qkv_rows = main.reshape(batch, t, -1)[:, max(t - (CONV_W - 1), 0):, :3 * GDN_HEADS * GDN_DK].astype(F32)
    new_conv = jnp.concatenate([conv0, qkv_rows], axis=1)[:, -(CONV_W - 1):]

    p_len = past_lat.shape[1]
    cos, sin = _rope_tables(p_len, t)
    cos2 = jnp.concatenate([cos, cos], axis=1)
    sin2 = jnp.concatenate([-sin, sin], axis=1)
    qn_w2 = qn_w.reshape(1, Q_LORA)
    kvn_w2 = kvn_w.reshape(1, KV_LORA)
    if p_len == 0:
        wq_t = jnp.transpose(w_q_up, (1, 2, 0)).astype(BF16)
        w_uk2 = w_uk.reshape(KV_LORA, hh * NOPE_DIM).astype(BF16)
        wuv_t = jnp.transpose(w_uv, (1, 2, 0)).astype(BF16)
        tmq = _pick(t, (512, 256, 128))
        q_t, k_all, v_t, lat, kr = _mla_prep(main, small, cos.T, sin.T, cos2, sin2, qn_w2, kvn_w2,
                                             wq_t, w_uk2, wuv_t, batch, t, tmq)
        tq = _pick(t, (1024, 512, 256))
        o_m = _flash(q_t, k_all, v_t, main, batch, t, tq, tq // 4)
    else:
        wq_n = jnp.transpose(w_q_up[:, :, :NOPE_DIM], (1, 0, 2)).astype(BF16)
        wq_r = jnp.transpose(w_q_up[:, :, NOPE_DIM:], (1, 0, 2))
        wq_rr = jnp.concatenate([wq_r[..., half:], wq_r[..., :half]], axis=-1).astype(BF16)
        wuk_t = jnp.transpose(w_uk, (1, 2, 0)).astype(BF16)
        wuv = jnp.transpose(w_uv, (1, 0, 2)).astype(BF16)
        o_m, lat, kr = _mla_decode(main, small, past_lat, past_kr, cos2, sin2, qn_w2, kvn_w2,
                                   wq_n, wq_r.astype(BF16), wq_rr, wuk_t, wuv, batch, t)

    w_g = w_out[:GDN_HEADS * GDN_DV].astype(BF16)
    w_m = w_out[GDN_HEADS * GDN_DV:].astype(BF16)
    y = _out_proj(x2d, o_g, o_m, w_g, w_m, lnf_w.reshape(1, d), _pick(n, (512, 256)))
    return y.reshape(batch, t, d), new_conv, new_ssm, lat, kr


def kernel(x_prompt, x_sample, cache_mla_latent, cache_mla_krope, state_gdn_conv, state_gdn_ssm, ln_in_w, w_in, w_conv, a_log, dt_bias, gdn_norm_w, q_norm_w, w_q_up, kv_norm_w, w_uk, w_uv, w_out, ln_final_w):
    depth = w_in.shape[0]
    assert depth == 1, "the final norm is fused into the out-projection of the single layer"
    bp = x_prompt.shape[0]
    lw = (ln_in_w[0], w_in[0], w_conv[0], a_log[0], dt_bias[0], gdn_norm_w[0],
          q_norm_w[0], w_q_up[0], kv_norm_w[0], w_uk[0], w_uv[0], w_out[0])
    zc = jnp.zeros((bp, CONV_W - 1, 3 * GDN_HEADS * GDN_DK), F32)
    zs = jnp.zeros((bp, GDN_HEADS, GDN_DK, GDN_DV), F32)
    yp, c1, s1, l1, r1 = _layer(x_prompt, zc, zs, jnp.zeros((bp, 0, KV_LORA), F32),
                                jnp.zeros((bp, 0, ROPE_DIM), F32), lw, ln_final_w)
    ys, c2, s2, l2, r2 = _layer(x_sample, state_gdn_conv[0], state_gdn_ssm[0], cache_mla_latent[0],
                                cache_mla_krope[0], lw, ln_final_w)
    return (yp, ys, c1[None], s1[None], l1[None], r1[None], c2[None], s2[None], l2[None], r2[None])
```

```python
import functools
import math

import jax
import jax.numpy as jnp
from jax import lax
from jax.experimental import pallas as pl
from jax.experimental.pallas import tpu as pltpu

F32 = jnp.float32
BF16 = jnp.bfloat16

EPS = 1e-6
CHUNK = 64
GDN_CHUNK = 64
GDN_HEADS = 8
GDN_DK = 128
GDN_DV = 128
CONV_W = 4
MLA_HEADS = 8
Q_LORA = 512
KV_LORA = 512
NOPE_DIM = 128
ROPE_DIM = 64
V_DIM = 128
ROPE_THETA = 10000.0
QK_DIM = NOPE_DIM + ROPE_DIM
V_ROWS = V_DIM + 16

LANE = 128
MXU_COLS = 256
VMEM_LIMIT = 56 * 1024 * 1024

_QB, _KB, _VB, _ZGB, _ZMB = 0, 8, 16, 24, 32
_CQ_BLK, _CKV_BLK = 10, 11
_MAIN_COLS = 6144
_SMALL_COLS = 256


def _nt(a, b):
    return lax.dot_general(a, b, (((1,), (1,)), ((), ())), preferred_element_type=F32)


def _tn(a, b):
    return lax.dot_general(a, b, (((0,), (0,)), ((), ())), preferred_element_type=F32)


def _dot(a, b):
    return jnp.dot(a, b, preferred_element_type=F32)


def _rms(x, w):
    return x * lax.rsqrt(jnp.mean(x * x, axis=-1, keepdims=True) + EPS) * w


def _sigmoid(x):
    return 1.0 / (1.0 + jnp.exp(-x))


def _params(*sem):
    return pltpu.CompilerParams(dimension_semantics=sem, vmem_limit_bytes=VMEM_LIMIT)


def _in_proj_kernel(x_ref, lnw_ref, wm_ref, ws_ref, main_ref, small_ref, h_ref, *, rows):
    @pl.when(pl.program_id(1) == 0)
    def _():
        def body(r, c):
            sl = pl.ds(pl.multiple_of(r * rows, rows), rows)
            h_ref[sl, :] = _rms(x_ref[sl, :], lnw_ref[...]).astype(BF16)
            return c
        lax.fori_loop(0, x_ref.shape[0] // rows, body, 0)
        small_ref[...] = _dot(h_ref[...], ws_ref[...])

    main_ref[...] = _dot(h_ref[...], wm_ref[...]).astype(BF16)


def _in_proj(x2d, ln_w, w_main, w_small, tm, tn):
    n, d = x2d.shape
    nm, ns = w_main.shape[1], w_small.shape[1]
    rows = min(tm, 128)
    return pl.pallas_call(
        functools.partial(_in_proj_kernel, rows=rows),
        grid=(n // tm, nm // tn),
        in_specs=[
            pl.BlockSpec((tm, d), lambda i, j: (i, 0)),
            pl.BlockSpec((1, d), lambda i, j: (0, 0)),
            pl.BlockSpec((d, tn), lambda i, j: (0, j)),
            pl.BlockSpec((d, ns), lambda i, j: (0, 0)),
        ],
        out_specs=[
            pl.BlockSpec((tm, tn), lambda i, j: (i, j)),
            pl.BlockSpec((tm, ns), lambda i, j: (i, 0)),
        ],
        out_shape=[
            jax.ShapeDtypeStruct((n, nm), BF16),
            jax.ShapeDtypeStruct((n, ns), F32),
        ],
        scratch_shapes=[pltpu.VMEM((tm, d), BF16)],
        compiler_params=_params("parallel", "arbitrary"),
        name="in_proj",
    )(x2d, ln_w, w_main, w_small)


def _out_proj_kernel(x_ref, og_ref, om_ref, wg_ref, wm_ref, lnf_ref, y_ref):
    acc = _dot(og_ref[...], wg_ref[...]) + _dot(om_ref[...], wm_ref[...])
    y_ref[...] = _rms(x_ref[...] + acc, lnf_ref[...])


def _out_proj(x2d, o_g, o_m, w_g, w_m, lnf_w, tm):
    n, d = x2d.shape
    kg, km = o_g.shape[1], o_m.shape[1]
    return pl.pallas_call(
        _out_proj_kernel,
        grid=(n // tm,),
        in_specs=[
            pl.BlockSpec((tm, d), lambda i: (i, 0)),
            pl.BlockSpec((tm, kg), lambda i: (i, 0)),
            pl.BlockSpec((tm, km), lambda i: (i, 0)),
            pl.BlockSpec((kg, d), lambda i: (0, 0)),
            pl.BlockSpec((km, d), lambda i: (0, 0)),
            pl.BlockSpec((1, d), lambda i: (0, 0)),
        ],
        out_specs=pl.BlockSpec((tm, d), lambda i: (i, 0)),
        out_shape=jax.ShapeDtypeStruct((n, d), F32),
        compiler_params=_params("parallel"),
        name="out_proj",
    )(x2d, o_g, o_m, w_g, w_m, lnf_w)


def _gdn_kernel(q_ref, k_ref, v_ref, z_ref, sm_ref, cq0_ref, ck0_ref, cv0_ref, s0_ref,
                wq_ref, wk_ref, wv_ref, gp_ref, gn_ref, o_ref, sout_ref, xbuf, s_ref, *, tb, c, hb):
    hg = pl.program_id(1)
    t = pl.program_id(2)
    dk = GDN_DK
    wd = hb * dk

    @pl.when(t == 0)
    def _():
        xbuf[0:8, :] = jnp.zeros((8, 3 * wd), F32)
        xbuf[8 - (CONV_W - 1):8, 0:wd] = cq0_ref[0]
        xbuf[8 - (CONV_W - 1):8, wd:2 * wd] = ck0_ref[0]
        xbuf[8 - (CONV_W - 1):8, 2 * wd:3 * wd] = cv0_ref[0]
        s_ref[...] = s0_ref[0]

    xbuf[8:8 + tb, 0:wd] = q_ref[...].astype(F32)
    xbuf[8:8 + tb, wd:2 * wd] = k_ref[...].astype(F32)
    xbuf[8:8 + tb, 2 * wd:3 * wd] = v_ref[...].astype(F32)

    srow = lax.broadcasted_iota(jnp.int32, (tb, tb), 0)
    scol = lax.broadcasted_iota(jnp.int32, (tb, tb), 1)
    shifts = [jnp.where(scol == srow - i, 1.0, 0.0).astype(BF16) for i in range(1, CONV_W)]
    gw = 2 if hb % 2 == 0 else 1
    row8 = lax.broadcasted_iota(jnp.int32, (8, gw * dk), 0)

    def conv(idx, x_ref, w_ref, hp):
        lo = hp * gw * dk
        w = w_ref[:, lo:lo + gw * dk]
        cols = slice(idx * wd + lo, idx * wd + lo + gw * dk)
        y = xbuf[8:8 + tb, cols] * w[CONV_W - 1:CONV_W]
        head = jnp.zeros((8, gw * dk), F32)
        for i in range(1, CONV_W):
            wi = w[CONV_W - 1 - i:CONV_W - i]
            y = y + _dot(shifts[i - 1], x_ref[:, lo:lo + gw * dk]) * wi
            head = head + jnp.where(row8 < i, xbuf[8 - i:16 - i, cols], 0.0) * wi
        y = jnp.concatenate([y[0:8] + head, y[8:]], axis=0)
        return y * _sigmoid(y)

    sm = sm_ref[...]
    gp = gp_ref[...]
    beta_all = _sigmoid(sm)
    xg = sm + gp[1:2]
    softplus = jnp.maximum(xg, 0.0) + jnp.log(1.0 + jnp.exp(-jnp.abs(xg)))
    g_all = -jnp.exp(gp[0:1]) * softplus
    lane = lax.broadcasted_iota(jnp.int32, (tb, LANE), 1)

    shift = int(math.log2(c))
    nc = tb // c
    row = lax.broadcasted_iota(jnp.int32, (tb, tb), 0)
    col = lax.broadcasted_iota(jnp.int32, (tb, tb), 1)
    same = jnp.right_shift(row, shift) == jnp.right_shift(col, shift)
    lincl = jnp.where(jnp.logical_and(same, col <= row), 1.0, 0.0).astype(BF16)
    bd_mask = jnp.where(same, 1.0, 0.0).astype(BF16)
    rl = lax.broadcasted_iota(jnp.int32, (c, tb), 0)
    ll = lax.broadcasted_iota(jnp.int32, (c, tb), 1)
    jl = jnp.bitwise_and(ll, c - 1)
    lb = jnp.right_shift(ll, shift)
    incl = jl <= rl
    strict = jl < rl
    eye = jnp.where(jl == rl, 1.0, 0.0)

    def to_ls(x):
        out = x[(nc - 1) * c:nc * c]
        for n in range(nc - 2, -1, -1):
            out = jnp.where(lb == n, x[n * c:(n + 1) * c], out)
        return out

    def to_bd(x_b):
        if nc == 1:
            return x_b
        return jnp.concatenate([x_b] * nc, axis=0) * bd_mask

    def widen(x):
        return x[:, :tb] if tb <= dk else jnp.concatenate([x] * (tb // dk), axis=1)

    hs = range(hb)
    beta = [jnp.sum(jnp.where(lane == hg * hb + h, beta_all, 0.0), axis=-1, keepdims=True) for h in hs]
    g = [jnp.sum(jnp.where(lane == hg * hb + h + GDN_HEADS, g_all, 0.0), axis=-1, keepdims=True) for h in hs]

    g_hi = [x.astype(BF16).astype(F32) for x in g]
    g_lo = [(x - y).astype(BF16).astype(F32) for x, y in zip(g, g_hi)]
    gsum = [_dot(lincl, jnp.concatenate([jnp.broadcast_to(x, (tb, dk)), jnp.broadcast_to(y, (tb, dk))],
                                        axis=1).astype(BF16)) for x, y in zip(g_hi, g_lo)]
    gcb = [x[:, :dk] + x[:, dk:] for x in gsum]
    gc_row = [x.T[0:1, :] for x in gcb]
    decay = [jnp.where(incl, jnp.exp(jnp.where(incl, to_ls(widen(x)) - y, 0.0)), 0.0)
             for x, y in zip(gcb, gc_row)]
    egc = [jnp.exp(x) for x in gcb]
    glast = [[x[(ci + 1) * c - 1:(ci + 1) * c, :] for ci in range(nc)] for x in gcb]
    cat = lambda xs: xs[0] if len(xs) == 1 else jnp.concatenate(xs, axis=0)
    kfac = [cat([jnp.exp(glast[h][ci] - gcb[h][ci * c:(ci + 1) * c]) for ci in range(nc)]) for h in hs]

    k_b, kb_b, kbg_b, q_b, qg_b, vb_b, kdec_b = ([None] * hb for _ in range(7))
    for hp in range(hb // gw):
        q2 = conv(0, q_ref, wq_ref, hp)
        k2 = conv(1, k_ref, wk_ref, hp)
        v2 = conv(2, v_ref, wv_ref, hp)
        for j in range(gw):
            h = hp * gw + j
            sl = slice(j * dk, (j + 1) * dk)
            qh, kh, vh = q2[:, sl], k2[:, sl], v2[:, sl]
            qh = qh * (lax.rsqrt(jnp.sum(qh * qh, axis=-1, keepdims=True) + EPS) * (dk ** -0.5))
            kh = kh * lax.rsqrt(jnp.sum(kh * kh, axis=-1, keepdims=True) + EPS)
            kbh = kh * beta[h]
            k_b[h] = kh.astype(BF16)
            kb_b[h] = kbh.astype(BF16)
            kbg_b[h] = (kbh * egc[h]).astype(BF16)
            q_b[h] = qh.astype(BF16)
            qg_b[h] = (qh * egc[h]).astype(BF16)
            vb_b[h] = (vh * beta[h]).astype(BF16)
            kdec_b[h] = (kh * kfac[h]).astype(BF16)

    kk = [to_ls(_nt(x, y)) for x, y in zip(kb_b, k_b)]
    p = [jnp.where(strict, -(x * y), 0.0) for x, y in zip(kk, decay)]
    tmat = [eye + x for x in p]
    p_bd = [to_bd(x.astype(BF16)) for x in p]
    for _ in range(shift - 1):
        p = [_dot(x.astype(BF16), y) for x, y in zip(p, p_bd)]
        p_bd = [to_bd(x.astype(BF16)) for x in p]
        tmat = [x + _dot(x.astype(BF16), y) for x, y in zip(tmat, p_bd)]
    t_bd = [to_bd(x.astype(BF16)) for x in tmat]
    uw = [_dot(x, jnp.concatenate([y, z], axis=1)) for x, y, z in zip(t_bd, vb_b, kbg_b)]
    u = [x[:, :GDN_DV] for x in uw]
    w_b = [x[:, GDN_DV:].astype(BF16) for x in uw]
    attn = [to_bd(jnp.where(incl, to_ls(_nt(x, y)) * z, 0.0).astype(BF16))
            for x, y, z in zip(q_b, k_b, decay)]

    s = [s_ref[h] for h in hs]
    vnews = [[] for _ in hs]
    inters = [[] for _ in hs]
    for ci in range(tb // c):
        rows = slice(ci * c, (ci + 1) * c)
        s_b = [x.astype(BF16) for x in s]
        v_new = [u[h][rows] - _dot(w_b[h][rows], s_b[h]) for h in hs]
        for h in hs:
            inters[h].append(_dot(qg_b[h][rows], s_b[h]))
            vnews[h].append(v_new[h])
        s = [s[h] * jnp.exp(glast[h][ci]) + _tn(kdec_b[h][rows], v_new[h].astype(BF16)) for h in hs]
    for h in hs:
        s_ref[h] = s[h]
        sout_ref[0, h] = s[h]
    o = [cat(inters[h]) + _dot(attn[h], cat(vnews[h]).astype(BF16)) for h in hs]
    for h in hs:
        z = z_ref[:, h * dk:(h + 1) * dk].astype(F32)
        o_ref[:, h * dk:(h + 1) * dk] = (_rms(o[h], gn_ref[...]) * (z * _sigmoid(z))).astype(BF16)

    xbuf[0:8, :] = xbuf[tb:tb + 8, :]


def _gdn(main, small, conv0, ssm0, w_conv, gate_rows, gnorm_w, batch, seq, tb, c, hb):
    n_t = seq // tb
    hh = GDN_HEADS
    dk = GDN_DK
    wd = hb * dk

    def blk(off):
        return pl.BlockSpec((tb, wd), lambda b, h, t: (b * n_t + t, off // hb + h))

    def cblk(off):
        return pl.BlockSpec((1, CONV_W - 1, wd), lambda b, h, t: (b, 0, off // hb + h))

    def wblk(off):
        return pl.BlockSpec((CONV_W, wd), lambda b, h, t: (0, off // hb + h))

    return pl.pallas_call(
        functools.partial(_gdn_kernel, tb=tb, c=c, hb=hb),
        grid=(batch, hh // hb, n_t),
        in_specs=[
            blk(_QB), blk(_KB), blk(_VB), blk(_ZGB),
            pl.BlockSpec((tb, LANE), lambda b, h, t: (b * n_t + t, 1)),
            cblk(_QB), cblk(_KB), cblk(_VB),
            pl.BlockSpec((1, hb, dk, GDN_DV), lambda b, h, t: (b, h, 0, 0)),
            wblk(_QB), wblk(_KB), wblk(_VB),
            pl.BlockSpec((2, LANE), lambda b, h, t: (0, 0)),
            pl.BlockSpec((1, GDN_DV), lambda b, h, t: (0, 0)),
        ],
        out_specs=[
            pl.BlockSpec((tb, wd), lambda b, h, t: (b * n_t + t, h)),
            pl.BlockSpec((1, hb, dk, GDN_DV), lambda b, h, t: (b, h, 0, 0)),
        ],
        out_shape=[
            jax.ShapeDtypeStruct((batch * seq, hh * GDN_DV), BF16),
            jax.ShapeDtypeStruct((batch, hh, dk, GDN_DV), F32),
        ],
        scratch_shapes=[pltpu.VMEM((tb + 8, 3 * wd), F32), pltpu.VMEM((hb, dk, GDN_DV), F32)],
        compiler_params=_params("parallel", "parallel", "arbitrary"),
        name="gdn",
    )(main, main, main, main, small, conv0, conv0, conv0, ssm0, w_conv, w_conv, w_conv, gate_rows, gnorm_w)


def _mla_prep_kernel(cq_ref, ckv_ref, sm_ref, cosq_ref, sinq_ref, cosk_ref, sink_ref, qnw_ref, kvnw_ref,
                     wqt_ref, wuk_ref, wuvt_ref, qt_ref, k_ref, vt_ref, lat_ref, kr_ref):
    scale = QK_DIM ** -0.5 * math.log2(math.e)
    half = ROPE_DIM // 2
    tm = cq_ref.shape[0]
    cqn = _rms(cq_ref[...].astype(F32), qnw_ref[...]).astype(BF16)
    ckvn32 = _rms(ckv_ref[...].astype(F32), kvnw_ref[...])
    lat_ref[0] = ckvn32
    ckvn = ckvn32.astype(BF16)
    sm = sm_ref[...]
    krope32 = sm[:, 0:ROPE_DIM] * cosk_ref[...] + sm[:, ROPE_DIM:2 * ROPE_DIM] * sink_ref[...]
    kr_ref[0] = krope32
    krope = krope32.astype(BF16)
    knope = _dot(ckvn, wuk_ref[...])
    cq_cos = cosq_ref[...]
    cq_sin = sinq_ref[...]
    for h in range(MLA_HEADS):
        k_ref[0, h, :, 0:NOPE_DIM] = knope[:, h * NOPE_DIM:(h + 1) * NOPE_DIM].astype(BF16)
        k_ref[0, h, :, NOPE_DIM:QK_DIM] = krope
        vt_ref[0, h, 0:V_DIM, :] = _nt(wuvt_ref[h], ckvn).astype(BF16)
        vt_ref[0, h, V_DIM:V_ROWS, :] = jnp.ones((V_ROWS - V_DIM, tm), BF16)
        qt = _nt(wqt_ref[h], cqn)
        x1 = qt[NOPE_DIM:NOPE_DIM + half]
        x2 = qt[NOPE_DIM + half:QK_DIM]
        qt_ref[0, h, 0:NOPE_DIM, :] = (qt[0:NOPE_DIM] * scale).astype(BF16)
        qt_ref[0, h, NOPE_DIM:NOPE_DIM + half, :] = ((x1 * cq_cos - x2 * cq_sin) * scale).astype(BF16)
        qt_ref[0, h, NOPE_DIM + half:QK_DIM, :] = ((x2 * cq_cos + x1 * cq_sin) * scale).astype(BF16)


def _mla_prep(main, small, cos_t, sin_t, cos2, sin2, qn_w, kvn_w, wq_t, w_uk2, wuv_t, batch, seq, tm):
    n_t = seq // tm
    hh = MLA_HEADS
    half = ROPE_DIM // 2
    const = lambda *shape: pl.BlockSpec(shape, lambda b, i: (0,) * len(shape))
    return pl.pallas_call(
        _mla_prep_kernel,
        grid=(batch, n_t),
        in_specs=[
            pl.BlockSpec((tm, Q_LORA), lambda b, i: (b * n_t + i, _CQ_BLK)),
            pl.BlockSpec((tm, KV_LORA), lambda b, i: (b * n_t + i, _CKV_BLK)),
            pl.BlockSpec((tm, LANE), lambda b, i: (b * n_t + i, 0)),
            pl.BlockSpec((half, tm), lambda b, i: (0, i)),
            pl.BlockSpec((half, tm), lambda b, i: (0, i)),
            pl.BlockSpec((tm, ROPE_DIM), lambda b, i: (i, 0)),
            pl.BlockSpec((tm, ROPE_DIM), lambda b, i: (i, 0)),
            const(1, Q_LORA), const(1, KV_LORA),
            const(hh, QK_DIM, Q_LORA), const(KV_LORA, hh * NOPE_DIM), const(hh, V_DIM, KV_LORA),
        ],
        out_specs=[
            pl.BlockSpec((1, hh, QK_DIM, tm), lambda b, i: (b, 0, 0, i)),
            pl.BlockSpec((1, hh, tm, QK_DIM), lambda b, i: (b, 0, i, 0)),
            pl.BlockSpec((1, hh, V_ROWS, tm), lambda b, i: (b, 0, 0, i)),
            pl.BlockSpec((1, tm, KV_LORA), lambda b, i: (b, i, 0)),
            pl.BlockSpec((1, tm, ROPE_DIM), lambda b, i: (b, i, 0)),
        ],
        out_shape=[
            jax.ShapeDtypeStruct((batch, hh, QK_DIM, seq), BF16),
            jax.ShapeDtypeStruct((batch, hh, seq, QK_DIM), BF16),
            jax.ShapeDtypeStruct((batch, hh, V_ROWS, seq), BF16),
            jax.ShapeDtypeStruct((batch, seq, KV_LORA), F32),
            jax.ShapeDtypeStruct((batch, seq, ROPE_DIM), F32),
        ],
        compiler_params=_params("parallel", "parallel"),
        name="mla_prep",
    )(main, main, small, cos_t, sin_t, cos2, sin2, qn_w, kvn_w, wq_t, w_uk2, wuv_t)


_NEG = -1e30


def _flash_kernel(qt_ref, k_ref, vt_ref, z_ref, bias_ref, o_ref, acc_ref, m_ref, s_buf, p_buf, mb_buf, al_buf,
                  *, tq, tk, unroll):
    r = tq // tk
    strip = min(tk, MXU_COLS)
    i = pl.program_id(2)
    nb = r * (i + 1)

    m_ref[...] = jnp.full(m_ref.shape, _NEG, F32)
    acc_ref[...] = jnp.zeros(acc_ref.shape, F32)
    p_buf[...] = jnp.zeros(p_buf.shape, BF16)
    al_buf[...] = jnp.ones(al_buf.shape, F32)
    s_buf[1] = jnp.full(s_buf.shape[1:], _NEG, F32)
    mb_buf[1] = jnp.full(mb_buf.shape[1:], _NEG, F32)

    def stage_a(blk, slot, u, cols):
        off = pl.multiple_of(blk * tk, tk)
        s = _dot(k_ref[0, 0, pl.ds(off, tk), :], qt_ref[0, 0, :, cols])
        if u >= 0 and cols.start == u * tk:
            s = s + bias_ref[...]
        s_buf[slot, :, cols] = s
        mb_buf[slot, :, cols] = jnp.max(s, axis=0, keepdims=True)

    def stage_b(slot, cols):
        m_prev = m_ref[:, cols]
        m_new = jnp.maximum(m_prev, mb_buf[slot, :, cols])
        al_buf[slot, :, cols] = jnp.exp2(m_prev - m_new)
        p_buf[slot, :, cols] = jnp.exp2(s_buf[slot, :, cols] - m_new).astype(BF16)
        m_ref[:, cols] = m_new

    def stage_c(blk, slot, cols):
        off = pl.multiple_of(jnp.maximum(blk, 0) * tk, tk)
        acc_ref[:, cols] = (acc_ref[:, cols] * al_buf[slot, :, cols]
                            + _dot(vt_ref[0, 0, :, pl.ds(off, tk)], p_buf[slot, :, cols]))

    def half_step(c, a, b):
        for c0 in range(0, tq, strip):
            cols = slice(c0, c0 + strip)
            if c is not None and c0 >= max(c[2], 0) * tk:
                stage_c(c[0], c[1], cols)
            if a is not None and c0 >= max(a[2], 0) * tk:
                stage_a(a[0], a[1], a[2], cols)
            if b is not None and c0 >= max(b[1], 0) * tk:
                stage_b(b[0], cols)

    def pair(e, u):
        half_step((e - 2, 0, u - 2), (e, 0, u), (1, u - 1))
        half_step((e - 1, 1, u - 1), (e + 1, 1, u + 1 if u >= 0 else u), (0, u))

    n_pairs = i * (r // 2)
    n_rem = n_pairs % unroll

    def body_rem(jp, carry):
        pair(2 * jp, -4)
        return carry

    def body(jp, carry):
        for x in range(unroll):
            pair(2 * (n_rem + unroll * jp + x), -4)
        return carry

    lax.fori_loop(0, n_rem, body_rem, 0)
    lax.fori_loop(0, n_pairs // unroll, body, 0)
    for x in range(r // 2):
        pair(nb - r + 2 * x, 2 * x)
    half_step((nb - 2, 0, r - 2), None, (1, r - 1))
    half_step((nb - 1, 1, r - 1), None, None)

    o = (acc_ref[0:V_DIM, :] / acc_ref[V_DIM:V_DIM + 1, :]).T
    z = z_ref[...].astype(F32)
    o_ref[...] = (o * (z * _sigmoid(z))).astype(BF16)


def _flash(q_t, k_all, v_t, main, batch, seq, tq, tk):
    n_q = seq // tq
    r = tq // tk
    unroll = 2
    assert r % 2 == 0 and tq % tk == 0
    hh = MLA_HEADS
    shift = int(math.log2(CHUNK))
    assert tk <= MXU_COLS
    cid = jnp.right_shift(jnp.arange(tk, dtype=jnp.int32), shift)
    bias = jnp.where(cid[:, None] <= cid[None, :], 0.0, _NEG).astype(F32)
    return pl.pallas_call(
        functools.partial(_flash_kernel, tq=tq, tk=tk, unroll=unroll),
        grid=(batch, hh, n_q),
        in_specs=[
            pl.BlockSpec((1, 1, QK_DIM, tq), lambda b, h, i: (b, h, 0, i)),
            pl.BlockSpec((1, 1, seq, QK_DIM), lambda b, h, i: (b, h, 0, 0)),
            pl.BlockSpec((1, 1, V_ROWS, seq), lambda b, h, i: (b, h, 0, 0)),
            pl.BlockSpec((tq, V_DIM), lambda b, h, i: (b * n_q + i, _ZMB + h)),
            pl.BlockSpec((tk, tk), lambda b, h, i: (0, 0)),
        ],
        out_specs=pl.BlockSpec((tq, V_DIM), lambda b, h, i: (b * n_q + i, h)),
        out_shape=jax.ShapeDtypeStruct((batch * seq, hh * V_DIM), BF16),
        scratch_shapes=[pltpu.VMEM((V_ROWS, tq), F32), pltpu.VMEM((1, tq), F32),
                        pltpu.VMEM((2, tk, tq), F32), pltpu.VMEM((2, tk, tq), BF16),
                        pltpu.VMEM((2, 1, tq), F32), pltpu.VMEM((2, 1, tq), F32)],
        compiler_params=_params("parallel", "parallel", "arbitrary"),
        name="flash",
    )(q_t, k_all, v_t, main, bias)


def _mla_decode_kernel(cq_ref, ckv_ref, sm_ref, z_ref, plat_ref, pkr_ref, cos_ref, sin_ref, qnw_ref, kvnw_ref,
                       wqn_ref, wqr_ref, wqrr_ref, wukt_ref, wuv_ref, o_ref, lat_ref, kr_ref, *, past, t):
    scale = QK_DIM ** -0.5
    cqn = _rms(cq_ref[...].astype(F32), qnw_ref[...]).astype(BF16)
    ckvn32 = _rms(ckv_ref[...].astype(F32), kvnw_ref[...])
    lat_ref[0] = ckvn32
    ckvn = ckvn32.astype(BF16)
    sm = sm_ref[...]
    cos2 = cos_ref[...]
    sin2 = sin_ref[...]
    krope32 = sm[:, 0:ROPE_DIM] * cos2 + sm[:, ROPE_DIM:2 * ROPE_DIM] * sin2
    kr_ref[0] = krope32
    krope = krope32.astype(BF16)

    qlats, qropes = [], []
    for h in range(MLA_HEADS):
        qn = _dot(cqn, wqn_ref[h]).astype(BF16)
        qlats.append((_dot(qn, wukt_ref[h]) * scale).astype(BF16))
        qr = _dot(cqn, wqr_ref[h]) * cos2 + _dot(cqn, wqrr_ref[h]) * sin2
        qropes.append((qr * scale).astype(BF16))
    qlat = jnp.concatenate(qlats, axis=0)
    qrope = jnp.concatenate(qropes, axis=0)

    plat = plat_ref[0].astype(BF16)
    pkr = pkr_ref[0].astype(BF16)
    s_past = _nt(qlat, plat) + _nt(qrope, pkr)
    s_new = _nt(qlat, ckvn) + _nt(qrope, krope)
    rows = MLA_HEADS * t
    qpos = past + jnp.bitwise_and(lax.broadcasted_iota(jnp.int32, (rows, t), 0), t - 1)
    kpos = past + lax.broadcasted_iota(jnp.int32, (rows, t), 1)
    shift = int(math.log2(CHUNK))
    visible = kpos < jnp.left_shift(jnp.right_shift(qpos, shift) + 1, shift)
    s_new = jnp.where(visible, s_new, -jnp.inf)
    m = jnp.maximum(jnp.max(s_past, axis=-1, keepdims=True), jnp.max(s_new, axis=-1, keepdims=True))
    p_past = jnp.exp(s_past - m)
    p_new = jnp.exp(s_new - m)
    l = jnp.sum(p_past, axis=-1, keepdims=True) + jnp.sum(p_new, axis=-1, keepdims=True)
    o_lat = (_dot(p_past.astype(BF16), plat) + _dot(p_new.astype(BF16), ckvn)) / l
    z = z_ref[...].astype(F32)
    gate = z * _sigmoid(z)
    for h in range(MLA_HEADS):
        o_h = _dot(o_lat[h * t:(h + 1) * t].astype(BF16), wuv_ref[h])
        cols = slice(h * V_DIM, (h + 1) * V_DIM)
        o_ref[:, cols] = (o_h * gate[:, cols]).astype(BF16)


def _mla_decode(main, small, past_lat, past_kr, cos2, sin2, qn_w, kvn_w, wq_n, wq_r, wq_rr, wuk_t, wuv, batch, t):
    past = past_lat.shape[1]
    hh = MLA_HEADS
    const = lambda *shape: pl.BlockSpec(shape, lambda b: (0,) * len(shape))
    return pl.pallas_call(
        functools.partial(_mla_decode_kernel, past=past, t=t),
        grid=(batch,),
        in_specs=[
            pl.BlockSpec((t, Q_LORA), lambda b: (b, _CQ_BLK)),
            pl.BlockSpec((t, KV_LORA), lambda b: (b, _CKV_BLK)),
            pl.BlockSpec((t, LANE), lambda b: (b, 0)),
            pl.BlockSpec((t, hh * V_DIM), lambda b: (b, _ZMB // hh)),
            pl.BlockSpec((1, past, KV_LORA), lambda b: (b, 0, 0)),
            pl.BlockSpec((1, past, ROPE_DIM), lambda b: (b, 0, 0)),
            const(t, ROPE_DIM), const(t, ROPE_DIM),
            const(1, Q_LORA), const(1, KV_LORA),
            const(hh, Q_LORA, NOPE_DIM), const(hh, Q_LORA, ROPE_DIM), const(hh, Q_LORA, ROPE_DIM),
            const(hh, NOPE_DIM, KV_LORA), const(hh, KV_LORA, V_DIM),
        ],
        out_specs=[
            pl.BlockSpec((t, hh * V_DIM), lambda b: (b, 0)),
            pl.BlockSpec((1, t, KV_LORA), lambda b: (b, 0, 0)),
            pl.BlockSpec((1, t, ROPE_DIM), lambda b: (b, 0, 0)),
        ],
        out_shape=[
            jax.ShapeDtypeStruct((batch * t, hh * V_DIM), BF16),
            jax.ShapeDtypeStruct((batch, t, KV_LORA), F32),
            jax.ShapeDtypeStruct((batch, t, ROPE_DIM), F32),
        ],
        compiler_params=_params("parallel"),
        name="mla_decode",
    )(main, main, small, main, past_lat, past_kr, cos2, sin2, qn_w, kvn_w, wq_n, wq_r, wq_rr, wuk_t, wuv)


def _rope_tables(p0, t):
    pos = jnp.arange(p0, p0 + t, dtype=jnp.int32)
    inv = ROPE_THETA ** (-jnp.arange(0, ROPE_DIM, 2, dtype=F32) / ROPE_DIM)
    ang = pos.astype(F32)[:, None] * inv[None, :]
    return jnp.cos(ang), jnp.sin(ang)


def _pick(n, prefs):
    for p in prefs:
        if n % p == 0:
            return p
    return n


def _layer(x, conv0, ssm0, past_lat, past_kr, lw, lnf_w):
    (ln_w, w_in, w_conv, a_log, dt_bias, gn_w, qn_w, w_q_up, kvn_w, w_uk, w_uv, w_out) = lw
    batch, t, d = x.shape
    n = batch * t
    hh = MLA_HEADS
    half = ROPE_DIM // 2
    x2d = x.reshape(n, d)

    sizes = (3 * GDN_HEADS * GDN_DK, GDN_HEADS * GDN_DV, GDN_HEADS, GDN_HEADS, Q_LORA, KV_LORA, ROPE_DIM, hh * V_DIM)
    offs = [0]
    for s in sizes:
        offs.append(offs[-1] + s)
    w_qkv, w_zg, w_bg, w_ag, w_cq, w_ckv, w_kr, w_zm = (w_in[:, offs[i]:offs[i + 1]] for i in range(8))
    w_main = jnp.concatenate([w_qkv, w_zg, w_zm, w_cq, w_ckv], axis=1).astype(BF16)
    w_small = jnp.concatenate(
        [w_kr, w_kr[:, half:], w_kr[:, :half], w_bg, w_ag,
         jnp.zeros((d, _SMALL_COLS - 2 * ROPE_DIM - 2 * GDN_HEADS), F32)], axis=1).astype(BF16)
    pad = jnp.zeros((LANE - 2 * GDN_HEADS,), F32)
    gate_rows = jnp.stack([jnp.concatenate([jnp.zeros((GDN_HEADS,), F32), a_log, pad]),
                           jnp.concatenate([jnp.zeros((GDN_HEADS,), F32), dt_bias, pad])])

    tm = _pick(n, (1024, 512, 256))
    main, small = _in_proj(x2d, ln_w.reshape(1, d), w_main, w_small, tm, 1536)

    c = min(GDN_CHUNK, t)
    tb = _pick(t, (256, 128, 64))
    o_g, new_ssm = _gdn(main, small, conv0, ssm0, w_conv, gate_rows, gn_w.reshape(1, GDN_DV), batch, t, tb, c, 8)
    qkv_rows = main.reshape(batch, t, -1)[:, max(t - (CONV_W - 1), 0):, :3 * GDN_HEADS * GDN_DK].astype(F32)
    new_conv = jnp.concatenate([conv0, qkv_rows], axis=1)[:, -(CONV_W - 1):]

    p_len = past_lat.shape[1]
    cos, sin = _rope_tables(p_len, t)
    cos2 = jnp.concatenate([cos, cos], axis=1)
    sin2 = jnp.concatenate([-sin, sin], axis=1)
    qn_w2 = qn_w.reshape(1, Q_LORA)
    kvn_w2 = kvn_w.reshape(1, KV_LORA)
    if p_len == 0:
        wq_t = jnp.transpose(w_q_up, (1, 2, 0)).astype(BF16)
        w_uk2 = w_uk.reshape(KV_LORA, hh * NOPE_DIM).astype(BF16)
        wuv_t = jnp.transpose(w_uv, (1, 2, 0)).astype(BF16)
        tmq = _pick(t, (512, 256, 128))
        q_t, k_all, v_t, lat, kr = _mla_prep(main, small, cos.T, sin.T, cos2, sin2, qn_w2, kvn_w2,
                                             wq_t, w_uk2, wuv_t, batch, t, tmq)
        tq = _pick(t, (1024, 512, 256))
        o_m = _flash(q_t, k_all, v_t, main, batch, t, tq, tq // 4)
    else:
        wq_n = jnp.transpose(w_q_up[:, :, :NOPE_DIM], (1, 0, 2)).astype(BF16)
        wq_r = jnp.transpose(w_q_up[:, :, NOPE_DIM:], (1, 0, 2))
        wq_rr = jnp.concatenate([wq_r[..., half:], wq_r[..., :half]], axis=-1).astype(BF16)
        wuk_t = jnp.transpose(w_uk, (1, 2, 0)).astype(BF16)
        wuv = jnp.transpose(w_uv, (1, 0, 2)).astype(BF16)
        o_m, lat, kr = _mla_decode(main, small, past_lat, past_kr, cos2, sin2, qn_w2, kvn_w2,
                                   wq_n, wq_r.astype(BF16), wq_rr, wuk_t, wuv, batch, t)

    w_g = w_out[:GDN_HEADS * GDN_DV].astype(BF16)
    w_m = w_out[GDN_HEADS * GDN_DV:].astype(BF16)
    y = _out_proj(x2d, o_g, o_m, w_g, w_m, lnf_w.reshape(1, d), _pick(n, (512, 256)))
    return y.reshape(batch, t, d), new_conv, new_ssm, lat, kr


def kernel(x_prompt, x_sample, cache_mla_latent, cache_mla_krope, state_gdn_conv, state_gdn_ssm, ln_in_w, w_in, w_conv, a_log, dt_bias, gdn_norm_w, q_norm_w, w_q_up, kv_norm_w, w_uk, w_uv, w_out, ln_final_w):
    depth = w_in.shape[0]
    assert depth == 1, "the final norm is fused into the out-projection of the single layer"
    bp = x_prompt.shape[0]
    lw = (ln_in_w[0], w_in[0], w_conv[0], a_log[0], dt_bias[0], gdn_norm_w[0],
          q_norm_w[0], w_q_up[0], kv_norm_w[0], w_uk[0], w_uv[0], w_out[0])
    zc = jnp.zeros((bp, CONV_W - 1, 3 * GDN_HEADS * GDN_DK), F32)
    zs = jnp.zeros((bp, GDN_HEADS, GDN_DK, GDN_DV), F32)
    yp, c1, s1, l1, r1 = _layer(x_prompt, zc, zs, jnp.zeros((bp, 0, KV_LORA), F32),
                                jnp.zeros((bp, 0, ROPE_DIM), F32), lw, ln_final_w)
    ys, c2, s2, l2, r2 = _layer(x_sample, state_gdn_conv[0], state_gdn_ssm[0], cache_mla_latent[0],
                                cache_mla_krope[0], lw, ln_final_w)
    return (yp, ys, c1[None], s1[None], l1[None], r1[None], c2[None], s2[None], l2[None], r2[None])
```

```python
import functools
import math

import jax
import jax.numpy as jnp
from jax import lax
from jax.experimental import pallas as pl
from jax.experimental.pallas import tpu as pltpu

F32 = jnp.float32
BF16 = jnp.bfloat16

EPS = 1e-6
CHUNK = 64
GDN_CHUNK = 64
GDN_HEADS = 8
GDN_DK = 128
GDN_DV = 128
CONV_W = 4
MLA_HEADS = 8
Q_LORA = 512
KV_LORA = 512
NOPE_DIM = 128
ROPE_DIM = 64
V_DIM = 128
ROPE_THETA = 10000.0
QK_DIM = NOPE_DIM + ROPE_DIM
V_ROWS = V_DIM + 16

LANE = 128
MXU_COLS = 256
VMEM_LIMIT = 56 * 1024 * 1024

_QB, _KB, _VB, _ZGB, _ZMB = 0, 8, 16, 24, 32
_CQ_BLK, _CKV_BLK = 10, 11
_MAIN_COLS = 6144
_SMALL_COLS = 256


def _nt(a, b):
    return lax.dot_general(a, b, (((1,), (1,)), ((), ())), preferred_element_type=F32)


def _tn(a, b):
    return lax.dot_general(a, b, (((0,), (0,)), ((), ())), preferred_element_type=F32)


def _dot(a, b):
    return jnp.dot(a, b, preferred_element_type=F32)


def _rms(x, w):
    return x * lax.rsqrt(jnp.mean(x * x, axis=-1, keepdims=True) + EPS) * w


def _sigmoid(x):
    return 1.0 / (1.0 + jnp.exp(-x))


def _params(*sem):
    return pltpu.CompilerParams(dimension_semantics=sem, vmem_limit_bytes=VMEM_LIMIT)


def _in_proj_kernel(x_ref, lnw_ref, wm_ref, ws_ref, main_ref, small_ref, h_ref, *, rows):
    @pl.when(pl.program_id(1) == 0)
    def _():
        def body(r, c):
            sl = pl.ds(pl.multiple_of(r * rows, rows), rows)
            h_ref[sl, :] = _rms(x_ref[sl, :], lnw_ref[...]).astype(BF16)
            return c
        lax.fori_loop(0, x_ref.shape[0] // rows, body, 0)
        small_ref[...] = _dot(h_ref[...], ws_ref[...])

    main_ref[...] = _dot(h_ref[...], wm_ref[...]).astype(BF16)


def _in_proj(x2d, ln_w, w_main, w_small, tm, tn):
    n, d = x2d.shape
    nm, ns = w_main.shape[1], w_small.shape[1]
    rows = min(tm, 128)
    return pl.pallas_call(
        functools.partial(_in_proj_kernel, rows=rows),
        grid=(n // tm, nm // tn),
        in_specs=[
            pl.BlockSpec((tm, d), lambda i, j: (i, 0)),
            pl.BlockSpec((1, d), lambda i, j: (0, 0)),
            pl.BlockSpec((d, tn), lambda i, j: (0, j)),
            pl.BlockSpec((d, ns), lambda i, j: (0, 0)),
        ],
        out_specs=[
            pl.BlockSpec((tm, tn), lambda i, j: (i, j)),
            pl.BlockSpec((tm, ns), lambda i, j: (i, 0)),
        ],
        out_shape=[
            jax.ShapeDtypeStruct((n, nm), BF16),
            jax.ShapeDtypeStruct((n, ns), F32),
        ],
        scratch_shapes=[pltpu.VMEM((tm, d), BF16)],
        compiler_params=_params("parallel", "arbitrary"),
        name="in_proj",
    )(x2d, ln_w, w_main, w_small)


def _out_proj_kernel(x_ref, og_ref, om_ref, wg_ref, wm_ref, lnf_ref, y_ref):
    acc = _dot(og_ref[...], wg_ref[...]) + _dot(om_ref[...], wm_ref[...])
    y_ref[...] = _rms(x_ref[...] + acc, lnf_ref[...])


def _out_proj(x2d, o_g, o_m, w_g, w_m, lnf_w, tm):
    n, d = x2d.shape
    kg, km = o_g.shape[1], o_m.shape[1]
    return pl.pallas_call(
        _out_proj_kernel,
        grid=(n // tm,),
        in_specs=[
            pl.BlockSpec((tm, d), lambda i: (i, 0)),
            pl.BlockSpec((tm, kg), lambda i: (i, 0)),
            pl.BlockSpec((tm, km), lambda i: (i, 0)),
            pl.BlockSpec((kg, d), lambda i: (0, 0)),
            pl.BlockSpec((km, d), lambda i: (0, 0)),
            pl.BlockSpec((1, d), lambda i: (0, 0)),
        ],
        out_specs=pl.BlockSpec((tm, d), lambda i: (i, 0)),
        out_shape=jax.ShapeDtypeStruct((n, d), F32),
        compiler_params=_params("parallel"),
        name="out_proj",
    )(x2d, o_g, o_m, w_g, w_m, lnf_w)


def _gdn_kernel(q_ref, k_ref, v_ref, z_ref, sm_ref, cq0_ref, ck0_ref, cv0_ref, s0_ref,
                wq_ref, wk_ref, wv_ref, gp_ref, gn_ref, o_ref, sout_ref, xbuf, s_ref, *, tb, c, hb):
    hg = pl.program_id(1)
    t = pl.program_id(2)
    dk = GDN_DK
    wd = hb * dk

    @pl.when(t == 0)
    def _():
        xbuf[0:8, :] = jnp.zeros((8, 3 * wd), F32)
        xbuf[8 - (CONV_W - 1):8, 0:wd] = cq0_ref[0]
        xbuf[8 - (CONV_W - 1):8, wd:2 * wd] = ck0_ref[0]
        xbuf[8 - (CONV_W - 1):8, 2 * wd:3 * wd] = cv0_ref[0]
        s_ref[...] = s0_ref[0]

    xbuf[8:8 + tb, 0:wd] = q_ref[...].astype(F32)
    xbuf[8:8 + tb, wd:2 * wd] = k_ref[...].astype(F32)
    xbuf[8:8 + tb, 2 * wd:3 * wd] = v_ref[...].astype(F32)

    srow = lax.broadcasted_iota(jnp.int32, (tb, tb), 0)
    scol = lax.broadcasted_iota(jnp.int32, (tb, tb), 1)
    shifts = [jnp.where(scol == srow - i, 1.0, 0.0).astype(BF16) for i in range(1, CONV_W)]
    gw = 2 if hb % 2 == 0 else 1
    row8 = lax.broadcasted_iota(jnp.int32, (8, gw * dk), 0)

    def conv(idx, x_ref, w_ref, hp):
        lo = hp * gw * dk
        w = w_ref[:, lo:lo + gw * dk]
        cols = slice(idx * wd + lo, idx * wd + lo + gw * dk)
        y = xbuf[8:8 + tb, cols] * w[CONV_W - 1:CONV_W]
        head = jnp.zeros((8, gw * dk), F32)
        for i in range(1, CONV_W):
            wi = w[CONV_W - 1 - i:CONV_W - i]
            y = y + _dot(shifts[i - 1], x_ref[:, lo:lo + gw * dk]) * wi
            head = head + jnp.where(row8 < i, xbuf[8 - i:16 - i, cols], 0.0) * wi
        y = jnp.concatenate([y[0:8] + head, y[8:]], axis=0)
        return y * _sigmoid(y)

    sm = sm_ref[...]
    gp = gp_ref[...]
    beta_all = _sigmoid(sm)
    xg = sm + gp[1:2]
    softplus = jnp.maximum(xg, 0.0) + jnp.log(1.0 + jnp.exp(-jnp.abs(xg)))
    g_all = -jnp.exp(gp[0:1]) * softplus
    lane = lax.broadcasted_iota(jnp.int32, (tb, LANE), 1)

    shift = int(math.log2(c))
    nc = tb // c
    row = lax.broadcasted_iota(jnp.int32, (tb, tb), 0)
    col = lax.broadcasted_iota(jnp.int32, (tb, tb), 1)
    same = jnp.right_shift(row, shift) == jnp.right_shift(col, shift)
    lincl = jnp.where(jnp.logical_and(same, col <= row), 1.0, 0.0).astype(BF16)
    bd_mask = jnp.where(same, 1.0, 0.0).astype(BF16)
    rl = lax.broadcasted_iota(jnp.int32, (c, tb), 0)
    ll = lax.broadcasted_iota(jnp.int32, (c, tb), 1)
    jl = jnp.bitwise_and(ll, c - 1)
    lb = jnp.right_shift(ll, shift)
    incl = jl <= rl
    strict = jl < rl
    eye = jnp.where(jl == rl, 1.0, 0.0)

    def to_ls(x):
        out = x[(nc - 1) * c:nc * c]
        for n in range(nc - 2, -1, -1):
            out = jnp.where(lb == n, x[n * c:(n + 1) * c], out)
        return out

    def to_bd(x_b):
        if nc == 1:
            return x_b
        return jnp.concatenate([x_b] * nc, axis=0) * bd_mask

    def widen(x):
        return x[:, :tb] if tb <= dk else jnp.concatenate([x] * (tb // dk), axis=1)

    hs = range(hb)
    beta = [jnp.sum(jnp.where(lane == hg * hb + h, beta_all, 0.0), axis=-1, keepdims=True) for h in hs]
    g = [jnp.sum(jnp.where(lane == hg * hb + h + GDN_HEADS, g_all, 0.0), axis=-1, keepdims=True) for h in hs]

    g_hi = [x.astype(BF16).astype(F32) for x in g]
    g_lo = [(x - y).astype(BF16).astype(F32) for x, y in zip(g, g_hi)]
    gsum = [_dot(lincl, jnp.concatenate([jnp.broadcast_to(x, (tb, dk)), jnp.broadcast_to(y, (tb, dk))],
                                        axis=1).astype(BF16)) for x, y in zip(g_hi, g_lo)]
    gcb = [x[:, :dk] + x[:, dk:] for x in gsum]
    gc_row = [x.T[0:1, :] for x in gcb]
    decay = [jnp.where(incl, jnp.exp(jnp.where(incl, to_ls(widen(x)) - y, 0.0)), 0.0)
             for x, y in zip(gcb, gc_row)]
    egc = [jnp.exp(x) for x in gcb]
    glast = [[x[(ci + 1) * c - 1:(ci + 1) * c, :] for ci in range(nc)] for x in gcb]
    cat = lambda xs: xs[0] if len(xs) == 1 else jnp.concatenate(xs, axis=0)
    kfac = [cat([jnp.exp(glast[h][ci] - gcb[h][ci * c:(ci + 1) * c]) for ci in range(nc)]) for h in hs]

    k_b, kb_b, kbg_b, q_b, qg_b, vb_b, kdec_b = ([None] * hb for _ in range(7))
    for hp in range(hb // gw):
        q2 = conv(0, q_ref, wq_ref, hp)
        k2 = conv(1, k_ref, wk_ref, hp)
        v2 = conv(2, v_ref, wv_ref, hp)
        for j in range(gw):
            h = hp * gw + j
            sl = slice(j * dk, (j + 1) * dk)
            qh, kh, vh = q2[:, sl], k2[:, sl], v2[:, sl]
            qh = qh * (lax.rsqrt(jnp.sum(qh * qh, axis=-1, keepdims=True) + EPS) * (dk ** -0.5))
            kh = kh * lax.rsqrt(jnp.sum(kh * kh, axis=-1, keepdims=True) + EPS)
            kbh = kh * beta[h]
            k_b[h] = kh.astype(BF16)
            kb_b[h] = kbh.astype(BF16)
            kbg_b[h] = (kbh * egc[h]).astype(BF16)
            q_b[h] = qh.astype(BF16)
            qg_b[h] = (qh * egc[h]).astype(BF16)
            vb_b[h] = (vh * beta[h]).astype(BF16)
            kdec_b[h] = (kh * kfac[h]).astype(BF16)

    kk = [to_ls(_nt(x, y)) for x, y in zip(kb_b, k_b)]
    p = [jnp.where(strict, -(x * y), 0.0) for x, y in zip(kk, decay)]
    tmat = [eye + x for x in p]
    p_bd = [to_bd(x.astype(BF16)) for x in p]
    for _ in range(shift - 1):
        p = [_dot(x.astype(BF16), y) for x, y in zip(p, p_bd)]
        p_bd = [to_bd(x.astype(BF16)) for x in p]
        tmat = [x + _dot(x.astype(BF16), y) for x, y in zip(tmat, p_bd)]
    t_bd = [to_bd(x.astype(BF16)) for x in tmat]
    uw = [_dot(x, jnp.concatenate([y, z], axis=1)) for x, y, z in zip(t_bd, vb_b, kbg_b)]
    u = [x[:, :GDN_DV] for x in uw]
    w_b = [x[:, GDN_DV:].astype(BF16) for x in uw]
    attn = [to_bd(jnp.where(incl, to_ls(_nt(x, y)) * z, 0.0).astype(BF16))
            for x, y, z in zip(q_b, k_b, decay)]

    s = [s_ref[h] for h in hs]
    vnews = [[] for _ in hs]
    inters = [[] for _ in hs]
    for ci in range(tb // c):
        rows = slice(ci * c, (ci + 1) * c)
        s_b = [x.astype(BF16) for x in s]
        v_new = [u[h][rows] - _dot(w_b[h][rows], s_b[h]) for h in hs]
        for h in hs:
            inters[h].append(_dot(qg_b[h][rows], s_b[h]))
            vnews[h].append(v_new[h])
        s = [s[h] * jnp.exp(glast[h][ci]) + _tn(kdec_b[h][rows], v_new[h].astype(BF16)) for h in hs]
    for h in hs:
        s_ref[h] = s[h]
        sout_ref[0, h] = s[h]
    o = [cat(inters[h]) + _dot(attn[h], cat(vnews[h]).astype(BF16)) for h in hs]
    for h in hs:
        z = z_ref[:, h * dk:(h + 1) * dk].astype(F32)
        o_ref[:, h * dk:(h + 1) * dk] = (_rms(o[h], gn_ref[...]) * (z * _sigmoid(z))).astype(BF16)

    xbuf[0:8, :] = xbuf[tb:tb + 8, :]


def _gdn(main, small, conv0, ssm0, w_conv, gate_rows, gnorm_w, batch, seq, tb, c, hb):
    n_t = seq // tb
    hh = GDN_HEADS
    dk = GDN_DK
    wd = hb * dk

    def blk(off):
        return pl.BlockSpec((tb, wd), lambda b, h, t: (b * n_t + t, off // hb + h))

    def cblk(off):
        return pl.BlockSpec((1, CONV_W - 1, wd), lambda b, h, t: (b, 0, off // hb + h))

    def wblk(off):
        return pl.BlockSpec((CONV_W, wd), lambda b, h, t: (0, off // hb + h))

    return pl.pallas_call(
        functools.partial(_gdn_kernel, tb=tb, c=c, hb=hb),
        grid=(batch, hh // hb, n_t),
        in_specs=[
            blk(_QB), blk(_KB), blk(_VB), blk(_ZGB),
            pl.BlockSpec((tb, LANE), lambda b, h, t: (b * n_t + t, 1)),
            cblk(_QB), cblk(_KB), cblk(_VB),
            pl.BlockSpec((1, hb, dk, GDN_DV), lambda b, h, t: (b, h, 0, 0)),
            wblk(_QB), wblk(_KB), wblk(_VB),
            pl.BlockSpec((2, LANE), lambda b, h, t: (0, 0)),
            pl.BlockSpec((1, GDN_DV), lambda b, h, t: (0, 0)),
        ],
        out_specs=[
            pl.BlockSpec((tb, wd), lambda b, h, t: (b * n_t + t, h)),
            pl.BlockSpec((1, hb, dk, GDN_DV), lambda b, h, t: (b, h, 0, 0)),
        ],
        out_shape=[
            jax.ShapeDtypeStruct((batch * seq, hh * GDN_DV), BF16),
            jax.ShapeDtypeStruct((batch, hh, dk, GDN_DV), F32),
        ],
        scratch_shapes=[pltpu.VMEM((tb + 8, 3 * wd), F32), pltpu.VMEM((hb, dk, GDN_DV), F32)],
        compiler_params=_params("parallel", "parallel", "arbitrary"),
        name="gdn",
    )(main, main, main, main, small, conv0, conv0, conv0, ssm0, w_conv, w_conv, w_conv, gate_rows, gnorm_w)


def _mla_prep_kernel(cq_ref, ckv_ref, sm_ref, cosq_ref, sinq_ref, cosk_ref, sink_ref, qnw_ref, kvnw_ref,
                     wqt_ref, wuk_ref, wuvt_ref, qt_ref, k_ref, vt_ref, lat_ref, kr_ref):
    scale = QK_DIM ** -0.5 * math.log2(math.e)
    half = ROPE_DIM // 2
    tm = cq_ref.shape[0]
    cqn = _rms(cq_ref[...].astype(F32), qnw_ref[...]).astype(BF16)
    ckvn32 = _rms(ckv_ref[...].astype(F32), kvnw_ref[...])
    lat_ref[0] = ckvn32
    ckvn = ckvn32.astype(BF16)
    sm = sm_ref[...]
    krope32 = sm[:, 0:ROPE_DIM] * cosk_ref[...] + sm[:, ROPE_DIM:2 * ROPE_DIM] * sink_ref[...]
    kr_ref[0] = krope32
    krope = krope32.astype(BF16)
    knope = _dot(ckvn, wuk_ref[...])
    cq_cos = cosq_ref[...]
    cq_sin = sinq_ref[...]
    vt_all = _nt(wuvt_ref[...], ckvn)
    qt_all = _nt(wqt_ref[...], cqn)
    for h in range(MLA_HEADS):
        k_ref[0, h, :, 0:NOPE_DIM] = knope[:, h * NOPE_DIM:(h + 1) * NOPE_DIM].astype(BF16)
        k_ref[0, h, :, NOPE_DIM:QK_DIM] = krope
        vt_ref[0, h, 0:V_DIM, :] = vt_all[h * V_DIM:(h + 1) * V_DIM].astype(BF16)
        vt_ref[0, h, V_DIM:V_ROWS, :] = jnp.ones((V_ROWS - V_DIM, tm), BF16)
        qt = qt_all[h * QK_DIM:(h + 1) * QK_DIM]
        x1 = qt[NOPE_DIM:NOPE_DIM + half]
        x2 = qt[NOPE_DIM + half:QK_DIM]
        qt_ref[0, h, 0:NOPE_DIM, :] = (qt[0:NOPE_DIM] * scale).astype(BF16)
        qt_ref[0, h, NOPE_DIM:NOPE_DIM + half, :] = ((x1 * cq_cos - x2 * cq_sin) * scale).astype(BF16)
        qt_ref[0, h, NOPE_DIM + half:QK_DIM, :] = ((x2 * cq_cos + x1 * cq_sin) * scale).astype(BF16)


def _mla_prep(main, small, cos_t, sin_t, cos2, sin2, qn_w, kvn_w, wq_t, w_uk2, wuv_t, batch, seq, tm):
    n_t = seq // tm
    hh = MLA_HEADS
    half = ROPE_DIM // 2
    const = lambda *shape: pl.BlockSpec(shape, lambda b, i: (0,) * len(shape))
    return pl.pallas_call(
        _mla_prep_kernel,
        grid=(batch, n_t),
        in_specs=[
            pl.BlockSpec((tm, Q_LORA), lambda b, i: (b * n_t + i, _CQ_BLK)),
            pl.BlockSpec((tm, KV_LORA), lambda b, i: (b * n_t + i, _CKV_BLK)),
            pl.BlockSpec((tm, LANE), lambda b, i: (b * n_t + i, 0)),
            pl.BlockSpec((half, tm), lambda b, i: (0, i)),
            pl.BlockSpec((half, tm), lambda b, i: (0, i)),
            pl.BlockSpec((tm, ROPE_DIM), lambda b, i: (i, 0)),
            pl.BlockSpec((tm, ROPE_DIM), lambda b, i: (i, 0)),
            const(1, Q_LORA), const(1, KV_LORA),
            const(hh * QK_DIM, Q_LORA), const(KV_LORA, hh * NOPE_DIM), const(hh * V_DIM, KV_LORA),
        ],
        out_specs=[
            pl.BlockSpec((1, hh, QK_DIM, tm), lambda b, i: (b, 0, 0, i)),
            pl.BlockSpec((1, hh, tm, QK_DIM), lambda b, i: (b, 0, i, 0)),
            pl.BlockSpec((1, hh, V_ROWS, tm), lambda b, i: (b, 0, 0, i)),
            pl.BlockSpec((1, tm, KV_LORA), lambda b, i: (b, i, 0)),
            pl.BlockSpec((1, tm, ROPE_DIM), lambda b, i: (b, i, 0)),
        ],
        out_shape=[
            jax.ShapeDtypeStruct((batch, hh, QK_DIM, seq), BF16),
            jax.ShapeDtypeStruct((batch, hh, seq, QK_DIM), BF16),
            jax.ShapeDtypeStruct((batch, hh, V_ROWS, seq), BF16),
            jax.ShapeDtypeStruct((batch, seq, KV_LORA), F32),
            jax.ShapeDtypeStruct((batch, seq, ROPE_DIM), F32),
        ],
        compiler_params=_params("parallel", "parallel"),
        name="mla_prep",
    )(main, main, small, cos_t, sin_t, cos2, sin2, qn_w, kvn_w, wq_t, w_uk2, wuv_t)


_NEG = -1e30


def _flash_kernel(qt_ref, k_ref, vt_ref, z_ref, bias_ref, o_ref, acc_ref, m_ref, s_buf, p_buf, mb_buf, al_buf,
                  *, tq, tk, unroll):
    r = tq // tk
    strip = min(tk, MXU_COLS)
    i = pl.program_id(2)
    nb = r * (i + 1)

    m_ref[...] = jnp.full(m_ref.shape, _NEG, F32)
    acc_ref[...] = jnp.zeros(acc_ref.shape, F32)
    p_buf[...] = jnp.zeros(p_buf.shape, BF16)
    al_buf[...] = jnp.ones(al_buf.shape, F32)
    s_buf[1] = jnp.full(s_buf.shape[1:], _NEG, F32)
    mb_buf[1] = jnp.full(mb_buf.shape[1:], _NEG, F32)

    def stage_a(blk, slot, u, cols):
        off = pl.multiple_of(blk * tk, tk)
        s = _dot(k_ref[0, 0, pl.ds(off, tk), :], qt_ref[0, 0, :, cols])
        if u >= 0 and cols.start < (u + 1) * tk:
            s = s + bias_ref[:, cols.start - u * tk:cols.stop - u * tk]
        s_buf[slot, :, cols] = s
        mb_buf[slot, :, cols] = jnp.max(s, axis=0, keepdims=True)

    def stage_b(slot, cols):
        m_prev = m_ref[:, cols]
        m_new = jnp.maximum(m_prev, mb_buf[slot, :, cols])
        al_buf[slot, :, cols] = jnp.exp2(m_prev - m_new)
        p_buf[slot, :, cols] = jnp.exp2(s_buf[slot, :, cols] - m_new).astype(BF16)
        m_ref[:, cols] = m_new

    def stage_c(blk, slot, cols):
        off = pl.multiple_of(jnp.maximum(blk, 0) * tk, tk)
        acc_ref[:, cols] = (acc_ref[:, cols] * al_buf[slot, :, cols]
                            + _dot(vt_ref[0, 0, :, pl.ds(off, tk)], p_buf[slot, :, cols]))

    def half_step(c, a, b):
        for c0 in range(0, tq, strip):
            cols = slice(c0, c0 + strip)
            if c is not None and c0 >= max(c[2], 0) * tk:
                stage_c(c[0], c[1], cols)
            if a is not None and c0 >= max(a[2], 0) * tk:
                stage_a(a[0], a[1], a[2], cols)
            if b is not None and c0 >= max(b[1], 0) * tk:
                stage_b(b[0], cols)

    def pair(e, u):
        half_step((e - 2, 0, u - 2), (e, 0, u), (1, u - 1))
        half_step((e - 1, 1, u - 1), (e + 1, 1, u + 1 if u >= 0 else u), (0, u))

    n_pairs = i * (r // 2)
    n_rem = n_pairs % unroll

    def body_rem(jp, carry):
        pair(2 * jp, -4)
        return carry

    def body(jp, carry):
        for x in range(unroll):
            pair(2 * (n_rem + unroll * jp + x), -4)
        return carry

    lax.fori_loop(0, n_rem, body_rem, 0)
    lax.fori_loop(0, n_pairs // unroll, body, 0)
    for x in range(r // 2):
        pair(nb - r + 2 * x, 2 * x)
    half_step((nb - 2, 0, r - 2), None, (1, r - 1))
    half_step((nb - 1, 1, r - 1), None, None)

    o = (acc_ref[0:V_DIM, :] / acc_ref[V_DIM:V_DIM + 1, :]).T
    z = z_ref[...].astype(F32)
    o_ref[...] = (o * (z * _sigmoid(z))).astype(BF16)


def _flash(q_t, k_all, v_t, main, batch, seq, tq, tk):
    n_q = seq // tq
    r = tq // tk
    unroll = 2
    assert r % 2 == 0 and tq % tk == 0
    hh = MLA_HEADS
    shift = int(math.log2(CHUNK))
    cid = jnp.right_shift(jnp.arange(tk, dtype=jnp.int32), shift)
    bias = jnp.where(cid[:, None] <= cid[None, :], 0.0, _NEG).astype(F32)
    return pl.pallas_call(
        functools.partial(_flash_kernel, tq=tq, tk=tk, unroll=unroll),
        grid=(batch, hh, n_q),
        in_specs=[
            pl.BlockSpec((1, 1, QK_DIM, tq), lambda b, h, i: (b, h, 0, i)),
            pl.BlockSpec((1, 1, seq, QK_DIM), lambda b, h, i: (b, h, 0, 0)),
            pl.BlockSpec((1, 1, V_ROWS, seq), lambda b, h, i: (b, h, 0, 0)),
            pl.BlockSpec((tq, V_DIM), lambda b, h, i: (b * n_q + i, _ZMB + h)),
            pl.BlockSpec((tk, tk), lambda b, h, i: (0, 0)),
        ],
        out_specs=pl.BlockSpec((tq, V_DIM), lambda b, h, i: (b * n_q + i, h)),
        out_shape=jax.ShapeDtypeStruct((batch * seq, hh * V_DIM), BF16),
        scratch_shapes=[pltpu.VMEM((V_ROWS, tq), F32), pltpu.VMEM((1, tq), F32),
                        pltpu.VMEM((2, tk, tq), F32), pltpu.VMEM((2, tk, tq), BF16),
                        pltpu.VMEM((2, 1, tq), F32), pltpu.VMEM((2, 1, tq), F32)],
        compiler_params=_params("parallel", "parallel", "arbitrary"),
        name="flash",
    )(q_t, k_all, v_t, main, bias)


def _mla_decode_kernel(cq_ref, ckv_ref, sm_ref, z_ref, plat_ref, pkr_ref, cos_ref, sin_ref, qnw_ref, kvnw_ref,
                       wqn_ref, wqr_ref, wqrr_ref, wukt_ref, wuv_ref, o_ref, lat_ref, kr_ref, *, past, t):
    scale = QK_DIM ** -0.5
    cqn = _rms(cq_ref[...].astype(F32), qnw_ref[...]).astype(BF16)
    ckvn32 = _rms(ckv_ref[...].astype(F32), kvnw_ref[...])
    lat_ref[0] = ckvn32
    ckvn = ckvn32.astype(BF16)
    sm = sm_ref[...]
    cos2 = cos_ref[...]
    sin2 = sin_ref[...]
    krope32 = sm[:, 0:ROPE_DIM] * cos2 + sm[:, ROPE_DIM:2 * ROPE_DIM] * sin2
    kr_ref[0] = krope32
    krope = krope32.astype(BF16)

    qlats, qropes = [], []
    for h in range(MLA_HEADS):
        qn = _dot(cqn, wqn_ref[h]).astype(BF16)
        qlats.append((_dot(qn, wukt_ref[h]) * scale).astype(BF16))
        qr = _dot(cqn, wqr_ref[h]) * cos2 + _dot(cqn, wqrr_ref[h]) * sin2
        qropes.append((qr * scale).astype(BF16))
    qlat = jnp.concatenate(qlats, axis=0)
    qrope = jnp.concatenate(qropes, axis=0)

    plat = plat_ref[0].astype(BF16)
    pkr = pkr_ref[0].astype(BF16)
    s_past = _nt(qlat, plat) + _nt(qrope, pkr)
    s_new = _nt(qlat, ckvn) + _nt(qrope, krope)
    rows = MLA_HEADS * t
    qpos = past + jnp.bitwise_and(lax.broadcasted_iota(jnp.int32, (rows, t), 0), t - 1)
    kpos = past + lax.broadcasted_iota(jnp.int32, (rows, t), 1)
    shift = int(math.log2(CHUNK))
    visible = kpos < jnp.left_shift(jnp.right_shift(qpos, shift) + 1, shift)
    s_new = jnp.where(visible, s_new, -jnp.inf)
    m = jnp.maximum(jnp.max(s_past, axis=-1, keepdims=True), jnp.max(s_new, axis=-1, keepdims=True))
    p_past = jnp.exp(s_past - m)
    p_new = jnp.exp(s_new - m)
    l = jnp.sum(p_past, axis=-1, keepdims=True) + jnp.sum(p_new, axis=-1, keepdims=True)
    o_lat = (_dot(p_past.astype(BF16), plat) + _dot(p_new.astype(BF16), ckvn)) / l
    z = z_ref[...].astype(F32)
    gate = z * _sigmoid(z)
    for h in range(MLA_HEADS):
        o_h = _dot(o_lat[h * t:(h + 1) * t].astype(BF16), wuv_ref[h])
        cols = slice(h * V_DIM, (h + 1) * V_DIM)
        o_ref[:, cols] = (o_h * gate[:, cols]).astype(BF16)


def _mla_decode(main, small, past_lat, past_kr, cos2, sin2, qn_w, kvn_w, wq_n, wq_r, wq_rr, wuk_t, wuv, batch, t):
    past = past_lat.shape[1]
    hh = MLA_HEADS
    const = lambda *shape: pl.BlockSpec(shape, lambda b: (0,) * len(shape))
    return pl.pallas_call(
        functools.partial(_mla_decode_kernel, past=past, t=t),
        grid=(batch,),
        in_specs=[
            pl.BlockSpec((t, Q_LORA), lambda b: (b, _CQ_BLK)),
            pl.BlockSpec((t, KV_LORA), lambda b: (b, _CKV_BLK)),
            pl.BlockSpec((t, LANE), lambda b: (b, 0)),
            pl.BlockSpec((t, hh * V_DIM), lambda b: (b, _ZMB // hh)),
            pl.BlockSpec((1, past, KV_LORA), lambda b: (b, 0, 0)),
            pl.BlockSpec((1, past, ROPE_DIM), lambda b: (b, 0, 0)),
            const(t, ROPE_DIM), const(t, ROPE_DIM),
            const(1, Q_LORA), const(1, KV_LORA),
            const(hh, Q_LORA, NOPE_DIM), const(hh, Q_LORA, ROPE_DIM), const(hh, Q_LORA, ROPE_DIM),
            const(hh, NOPE_DIM, KV_LORA), const(hh, KV_LORA, V_DIM),
        ],
        out_specs=[
            pl.BlockSpec((t, hh * V_DIM), lambda b: (b, 0)),
            pl.BlockSpec((1, t, KV_LORA), lambda b: (b, 0, 0)),
            pl.BlockSpec((1, t, ROPE_DIM), lambda b: (b, 0, 0)),
        ],
        out_shape=[
            jax.ShapeDtypeStruct((batch * t, hh * V_DIM), BF16),
            jax.ShapeDtypeStruct((batch, t, KV_LORA), F32),
            jax.ShapeDtypeStruct((batch, t, ROPE_DIM), F32),
        ],
        compiler_params=_params("parallel"),
        name="mla_decode",
    )(main, main, small, main, past_lat, past_kr, cos2, sin2, qn_w, kvn_w, wq_n, wq_r, wq_rr, wuk_t, wuv)


def _rope_tables(p0, t):
    pos = jnp.arange(p0, p0 + t, dtype=jnp.int32)
    inv = ROPE_THETA ** (-jnp.arange(0, ROPE_DIM, 2, dtype=F32) / ROPE_DIM)
    ang = pos.astype(F32)[:, None] * inv[None, :]
    return jnp.cos(ang), jnp.sin(ang)


def _pick(n, prefs):
    for p in prefs:
        if n % p == 0:
            return p
    return n


def _layer(x, conv0, ssm0, past_lat, past_kr, lw, lnf_w):
    (ln_w, w_in, w_conv, a_log, dt_bias, gn_w, qn_w, w_q_up, kvn_w, w_uk, w_uv, w_out) = lw
    batch, t, d = x.shape
    n = batch * t
    hh = MLA_HEADS
    half = ROPE_DIM // 2
    x2d = x.reshape(n, d)

    sizes = (3 * GDN_HEADS * GDN_DK, GDN_HEADS * GDN_DV, GDN_HEADS, GDN_HEADS, Q_LORA, KV_LORA, ROPE_DIM, hh * V_DIM)
    offs = [0]
    for s in sizes:
        offs.append(offs[-1] + s)
    w_qkv, w_zg, w_bg, w_ag, w_cq, w_ckv, w_kr, w_zm = (w_in[:, offs[i]:offs[i + 1]] for i in range(8))
    w_main = jnp.concatenate([w_qkv, w_zg, w_zm, w_cq, w_ckv], axis=1).astype(BF16)
    w_small = jnp.concatenate(
        [w_kr, w_kr[:, half:], w_kr[:, :half], w_bg, w_ag,
         jnp.zeros((d, _SMALL_COLS - 2 * ROPE_DIM - 2 * GDN_HEADS), F32)], axis=1).astype(BF16)
    pad = jnp.zeros((LANE - 2 * GDN_HEADS,), F32)
    gate_rows = jnp.stack([jnp.concatenate([jnp.zeros((GDN_HEADS,), F32), a_log, pad]),
                           jnp.concatenate([jnp.zeros((GDN_HEADS,), F32), dt_bias, pad])])

    tm = _pick(n, (1024, 512, 256))
    main, small = _in_proj(x2d, ln_w.reshape(1, d), w_main, w_small, tm, 1536)

    c = min(GDN_CHUNK, t)
    tb = _pick(t, (256, 128, 64))
    o_g, new_ssm = _gdn(main, small, conv0, ssm0, w_conv, gate_rows, gn_w.reshape(1, GDN_DV), batch, t, tb, c, 8)
    qkv_rows = main.reshape(batch, t, -1)[:, max(t - (CONV_W - 1), 0):, :3 * GDN_HEADS * GDN_DK].astype(F32)
    new_conv = jnp.concatenate([conv0, qkv_rows], axis=1)[:, -(CONV_W - 1):]

    p_len = past_lat.shape[1]
    cos, sin = _rope_tables(p_len, t)
    cos2 = jnp.concatenate([cos, cos], axis=1)
    sin2 = jnp.concatenate([-sin, sin], axis=1)
    qn_w2 = qn_w.reshape(1, Q_LORA)
    kvn_w2 = kvn_w.reshape(1, KV_LORA)
    if p_len == 0:
        wq_t = jnp.transpose(w_q_up, (1, 2, 0)).reshape(hh * QK_DIM, Q_LORA).astype(BF16)
        w_uk2 = w_uk.reshape(KV_LORA, hh * NOPE_DIM).astype(BF16)
        wuv_t = jnp.transpose(w_uv, (1, 2, 0)).reshape(hh * V_DIM, KV_LORA).astype(BF16)
        tmq = _pick(t, (512, 256, 128))
        q_t, k_all, v_t, lat, kr = _mla_prep(main, small, cos.T, sin.T, cos2, sin2, qn_w2, kvn_w2,
                                             wq_t, w_uk2, wuv_t, batch, t, tmq)
        tq = _pick(t, (1024, 512, 256))
        o_m = _flash(q_t, k_all, v_t, main, batch, t, tq, tq // 4)
    else:
        wq_n = jnp.transpose(w_q_up[:, :, :NOPE_DIM], (1, 0, 2)).astype(BF16)
        wq_r = jnp.transpose(w_q_up[:, :, NOPE_DIM:], (1, 0, 2))
        wq_rr = jnp.concatenate([wq_r[..., half:], wq_r[..., :half]], axis=-1).astype(BF16)
        wuk_t = jnp.transpose(w_uk, (1, 2, 0)).astype(BF16)
        wuv = jnp.transpose(w_uv, (1, 0, 2)).astype(BF16)
        o_m, lat, kr = _mla_decode(main, small, past_lat, past_kr, cos2, sin2, qn_w2, kvn_w2,
                                   wq_n, wq_r.astype(BF16), wq_rr, wuk_t, wuv, batch, t)

    w_g = w_out[:GDN_HEADS * GDN_DV].astype(BF16)
    w_m = w_out[GDN_HEADS * GDN_DV:].astype(BF16)
    y = _out_proj(x2d, o_g, o_m, w_g, w_m, lnf_w.reshape(1, d), _pick(n, (512, 256)))
    return y.reshape(batch, t, d), new_conv, new_ssm, lat, kr


def kernel(x_prompt, x_sample, cache_mla_latent, cache_mla_krope, state_gdn_conv, state_gdn_ssm, ln_in_w, w_in, w_conv, a_log, dt_bias, gdn_norm_w, q_norm_w, w_q_up, kv_norm_w, w_uk, w_uv, w_out, ln_final_w):
    depth = w_in.shape[0]
    assert depth == 1, "the final norm is fused into the out-projection of the single layer"
    bp = x_prompt.shape[0]
    lw = (ln_in_w[0], w_in[0], w_conv[0], a_log[0], dt_bias[0], gdn_norm_w[0],
          q_norm_w[0], w_q_up[0], kv_norm_w[0], w_uk[0], w_uv[0], w_out[0])
    zc = jnp.zeros((bp, CONV_W - 1, 3 * GDN_HEADS * GDN_DK), F32)
    zs = jnp.zeros((bp, GDN_HEADS, GDN_DK, GDN_DV), F32)
    yp, c1, s1, l1, r1 = _layer(x_prompt, zc, zs, jnp.zeros((bp, 0, KV_LORA), F32),
                                jnp.zeros((bp, 0, ROPE_DIM), F32), lw, ln_final_w)
    ys, c2, s2, l2, r2 = _layer(x_sample, state_gdn_conv[0], state_gdn_ssm[0], cache_mla_latent[0],
                                cache_mla_krope[0], lw, ln_final_w)
    return (yp, ys, c1[None], s1[None], l1[None], r1[None], c2[None], s2[None], l2[None], r2[None])
```

```python
import functools
import math

import jax
import jax.numpy as jnp
from jax import lax
from jax.experimental import pallas as pl
from jax.experimental.pallas import tpu as pltpu

F32 = jnp.float32
BF16 = jnp.bfloat16

EPS = 1e-6
CHUNK = 64
GDN_CHUNK = 64
GDN_HEADS = 8
GDN_DK = 128
GDN_DV = 128
CONV_W = 4
MLA_HEADS = 8
Q_LORA = 512
KV_LORA = 512
NOPE_DIM = 128
ROPE_DIM = 64
V_DIM = 128
ROPE_THETA = 10000.0
QK_DIM = NOPE_DIM + ROPE_DIM
V_ROWS = V_DIM + 16

LANE = 128
MXU_COLS = 256
VMEM_LIMIT = 56 * 1024 * 1024

_QB, _KB, _VB, _ZGB, _ZMB = 0, 8, 16, 24, 32
_CQ_BLK, _CKV_BLK = 10, 11
_MAIN_COLS = 6144
_SMALL_COLS = 256


def _nt(a, b):
    return lax.dot_general(a, b, (((1,), (1,)), ((), ())), preferred_element_type=F32)


def _tn(a, b):
    return lax.dot_general(a, b, (((0,), (0,)), ((), ())), preferred_element_type=F32)


def _dot(a, b):
    return jnp.dot(a, b, preferred_element_type=F32)


def _rms(x, w):
    return x * lax.rsqrt(jnp.mean(x * x, axis=-1, keepdims=True) + EPS) * w


def _sigmoid(x):
    return 1.0 / (1.0 + jnp.exp(-x))


def _params(*sem):
    return pltpu.CompilerParams(dimension_semantics=sem, vmem_limit_bytes=VMEM_LIMIT)


def _in_proj_kernel(x_ref, lnw_ref, wm_ref, ws_ref, main_ref, small_ref, h_ref, *, rows):
    @pl.when(pl.program_id(1) == 0)
    def _():
        def body(r, c):
            sl = pl.ds(pl.multiple_of(r * rows, rows), rows)
            h_ref[sl, :] = _rms(x_ref[sl, :], lnw_ref[...]).astype(BF16)
            return c
        lax.fori_loop(0, x_ref.shape[0] // rows, body, 0)
        small_ref[...] = _dot(h_ref[...], ws_ref[...])

    main_ref[...] = _dot(h_ref[...], wm_ref[...]).astype(BF16)


def _in_proj(x2d, ln_w, w_main, w_small, tm, tn):
    n, d = x2d.shape
    nm, ns = w_main.shape[1], w_small.shape[1]
    rows = min(tm, 128)
    return pl.pallas_call(
        functools.partial(_in_proj_kernel, rows=rows),
        grid=(n // tm, nm // tn),
        in_specs=[
            pl.BlockSpec((tm, d), lambda i, j: (i, 0)),
            pl.BlockSpec((1, d), lambda i, j: (0, 0)),
            pl.BlockSpec((d, tn), lambda i, j: (0, j)),
            pl.BlockSpec((d, ns), lambda i, j: (0, 0)),
        ],
        out_specs=[
            pl.BlockSpec((tm, tn), lambda i, j: (i, j)),
            pl.BlockSpec((tm, ns), lambda i, j: (i, 0)),
        ],
        out_shape=[
            jax.ShapeDtypeStruct((n, nm), BF16),
            jax.ShapeDtypeStruct((n, ns), F32),
        ],
        scratch_shapes=[pltpu.VMEM((tm, d), BF16)],
        compiler_params=_params("parallel", "arbitrary"),
        name="in_proj",
    )(x2d, ln_w, w_main, w_small)


def _out_proj_kernel(x_ref, og_ref, om_ref, wg_ref, wm_ref, lnf_ref, y_ref):
    acc = _dot(og_ref[...], wg_ref[...]) + _dot(om_ref[...], wm_ref[...])
    y_ref[...] = _rms(x_ref[...] + acc, lnf_ref[...])


def _out_proj(x2d, o_g, o_m, w_g, w_m, lnf_w, tm):
    n, d = x2d.shape
    kg, km = o_g.shape[1], o_m.shape[1]
    return pl.pallas_call(
        _out_proj_kernel,
        grid=(n // tm,),
        in_specs=[
            pl.BlockSpec((tm, d), lambda i: (i, 0)),
            pl.BlockSpec((tm, kg), lambda i: (i, 0)),
            pl.BlockSpec((tm, km), lambda i: (i, 0)),
            pl.BlockSpec((kg, d), lambda i: (0, 0)),
            pl.BlockSpec((km, d), lambda i: (0, 0)),
            pl.BlockSpec((1, d), lambda i: (0, 0)),
        ],
        out_specs=pl.BlockSpec((tm, d), lambda i: (i, 0)),
        out_shape=jax.ShapeDtypeStruct((n, d), F32),
        compiler_params=_params("parallel"),
        name="out_proj",
    )(x2d, o_g, o_m, w_g, w_m, lnf_w)


def _gdn_kernel(q_ref, k_ref, v_ref, z_ref, sm_ref, cq0_ref, ck0_ref, cv0_ref, s0_ref,
                wq_ref, wk_ref, wv_ref, gp_ref, gn_ref, o_ref, sout_ref, xbuf, s_ref, *, tb, c, hb):
    hg = pl.program_id(1)
    t = pl.program_id(2)
    dk = GDN_DK
    wd = hb * dk

    @pl.when(t == 0)
    def _():
        xbuf[0:8, :] = jnp.zeros((8, 3 * wd), F32)
        xbuf[8 - (CONV_W - 1):8, 0:wd] = cq0_ref[0]
        xbuf[8 - (CONV_W - 1):8, wd:2 * wd] = ck0_ref[0]
        xbuf[8 - (CONV_W - 1):8, 2 * wd:3 * wd] = cv0_ref[0]
        s_ref[...] = s0_ref[0]

    xbuf[8:8 + tb, 0:wd] = q_ref[...].astype(F32)
    xbuf[8:8 + tb, wd:2 * wd] = k_ref[...].astype(F32)
    xbuf[8:8 + tb, 2 * wd:3 * wd] = v_ref[...].astype(F32)

    srow = lax.broadcasted_iota(jnp.int32, (tb, tb), 0)
    scol = lax.broadcasted_iota(jnp.int32, (tb, tb), 1)
    shifts = [jnp.where(scol == srow - i, 1.0, 0.0).astype(BF16) for i in range(1, CONV_W)]
    gw = 2 if hb % 2 == 0 else 1
    row8 = lax.broadcasted_iota(jnp.int32, (8, gw * dk), 0)

    def conv(idx, x_ref, w_ref, hp):
        lo = hp * gw * dk
        w = w_ref[:, lo:lo + gw * dk]
        cols = slice(idx * wd + lo, idx * wd + lo + gw * dk)
        y = xbuf[8:8 + tb, cols] * w[CONV_W - 1:CONV_W]
        head = jnp.zeros((8, gw * dk), F32)
        for i in range(1, CONV_W):
            wi = w[CONV_W - 1 - i:CONV_W - i]
            y = y + _dot(shifts[i - 1], x_ref[:, lo:lo + gw * dk]) * wi
            head = head + jnp.where(row8 < i, xbuf[8 - i:16 - i, cols], 0.0) * wi
        y = jnp.concatenate([y[0:8] + head, y[8:]], axis=0)
        return y * _sigmoid(y)

    sm = sm_ref[...]
    gp = gp_ref[...]
    beta_all = _sigmoid(sm)
    xg = sm + gp[1:2]
    softplus = jnp.maximum(xg, 0.0) + jnp.log(1.0 + jnp.exp(-jnp.abs(xg)))
    g_all = -jnp.exp(gp[0:1]) * softplus
    lane = lax.broadcasted_iota(jnp.int32, (tb, LANE), 1)

    shift = int(math.log2(c))
    nc = tb // c
    row = lax.broadcasted_iota(jnp.int32, (tb, tb), 0)
    col = lax.broadcasted_iota(jnp.int32, (tb, tb), 1)
    same = jnp.right_shift(row, shift) == jnp.right_shift(col, shift)
    lincl = jnp.where(jnp.logical_and(same, col <= row), 1.0, 0.0).astype(BF16)
    bd_mask = jnp.where(same, 1.0, 0.0).astype(BF16)
    rl = lax.broadcasted_iota(jnp.int32, (c, tb), 0)
    ll = lax.broadcasted_iota(jnp.int32, (c, tb), 1)
    jl = jnp.bitwise_and(ll, c - 1)
    lb = jnp.right_shift(ll, shift)
    incl = jl <= rl
    strict = jl < rl
    eye = jnp.where(jl == rl, 1.0, 0.0)

    def to_ls(x):
        out = x[(nc - 1) * c:nc * c]
        for n in range(nc - 2, -1, -1):
            out = jnp.where(lb == n, x[n * c:(n + 1) * c], out)
        return out

    def to_bd(x_b):
        if nc == 1:
            return x_b
        return jnp.concatenate([x_b] * nc, axis=0) * bd_mask

    def widen(x):
        return x[:, :tb] if tb <= dk else jnp.concatenate([x] * (tb // dk), axis=1)

    hs = range(hb)
    beta = [jnp.sum(jnp.where(lane == hg * hb + h, beta_all, 0.0), axis=-1, keepdims=True) for h in hs]
    g = [jnp.sum(jnp.where(lane == hg * hb + h + GDN_HEADS, g_all, 0.0), axis=-1, keepdims=True) for h in hs]

    g_hi = [x.astype(BF16).astype(F32) for x in g]
    g_lo = [(x - y).astype(BF16).astype(F32) for x, y in zip(g, g_hi)]
    gsum = [_dot(lincl, jnp.concatenate([jnp.broadcast_to(x, (tb, dk)), jnp.broadcast_to(y, (tb, dk))],
                                        axis=1).astype(BF16)) for x, y in zip(g_hi, g_lo)]
    gcb = [x[:, :dk] + x[:, dk:] for x in gsum]
    gc_row = [x.T[0:1, :] for x in gcb]
    decay = [jnp.where(incl, jnp.exp(jnp.where(incl, to_ls(widen(x)) - y, 0.0)), 0.0)
             for x, y in zip(gcb, gc_row)]
    egc = [jnp.exp(x) for x in gcb]
    glast = [[x[(ci + 1) * c - 1:(ci + 1) * c, :] for ci in range(nc)] for x in gcb]
    cat = lambda xs: xs[0] if len(xs) == 1 else jnp.concatenate(xs, axis=0)
    kfac = [cat([jnp.exp(glast[h][ci] - gcb[h][ci * c:(ci + 1) * c]) for ci in range(nc)]) for h in hs]

    k_b, kb_b, kbg_b, q_b, qg_b, vb_b, kdec_b = ([None] * hb for _ in range(7))
    for hp in range(hb // gw):
        q2 = conv(0, q_ref, wq_ref, hp)
        k2 = conv(1, k_ref, wk_ref, hp)
        v2 = conv(2, v_ref, wv_ref, hp)
        for j in range(gw):
            h = hp * gw + j
            sl = slice(j * dk, (j + 1) * dk)
            qh, kh, vh = q2[:, sl], k2[:, sl], v2[:, sl]
            qh = qh * (lax.rsqrt(jnp.sum(qh * qh, axis=-1, keepdims=True) + EPS) * (dk ** -0.5))
            kh = kh * lax.rsqrt(jnp.sum(kh * kh, axis=-1, keepdims=True) + EPS)
            kbh = kh * beta[h]
            k_b[h] = kh.astype(BF16)
            kb_b[h] = kbh.astype(BF16)
            kbg_b[h] = (kbh * egc[h]).astype(BF16)
            q_b[h] = qh.astype(BF16)
            qg_b[h] = (qh * egc[h]).astype(BF16)
            vb_b[h] = (vh * beta[h]).astype(BF16)
            kdec_b[h] = (kh * kfac[h]).astype(BF16)

    kk = [to_ls(_nt(x, y)) for x, y in zip(kb_b, k_b)]
    p = [jnp.where(strict, -(x * y), 0.0) for x, y in zip(kk, decay)]
    tmat = [eye + x for x in p]
    p = [_dot(x.astype(BF16), to_bd(x.astype(BF16))) for x in p]
    for lvl in range(1, shift):
        p_b = [x.astype(BF16) for x in p]
        p_bd = [to_bd(x) for x in p_b]
        if lvl < shift - 1:
            both = [_dot(jnp.concatenate([x, y.astype(BF16)], axis=0), z) for x, y, z in zip(p_b, tmat, p_bd)]
            p = [x[:c] for x in both]
            tmat = [x + y[c:] for x, y in zip(tmat, both)]
        else:
            tmat = [x + _dot(x.astype(BF16), y) for x, y in zip(tmat, p_bd)]
    t_bd = [to_bd(x.astype(BF16)) for x in tmat]
    uw = [_dot(x, jnp.concatenate([y, z], axis=1)) for x, y, z in zip(t_bd, vb_b, kbg_b)]
    u = [x[:, :GDN_DV] for x in uw]
    w_b = [x[:, GDN_DV:].astype(BF16) for x in uw]
    attn = [to_bd(jnp.where(incl, to_ls(_nt(x, y)) * z, 0.0).astype(BF16))
            for x, y, z in zip(q_b, k_b, decay)]

    s = [s_ref[h] for h in hs]
    vnews = [[] for _ in hs]
    inters = [[] for _ in hs]
    for ci in range(tb // c):
        rows = slice(ci * c, (ci + 1) * c)
        s_b = [x.astype(BF16) for x in s]
        both = [_dot(jnp.concatenate([w_b[h][rows], qg_b[h][rows]], axis=0), s_b[h]) for h in hs]
        v_new = [u[h][rows] - both[h][:c] for h in hs]
        for h in hs:
            inters[h].append(both[h][c:])
            vnews[h].append(v_new[h])
        s = [s[h] * jnp.exp(glast[h][ci]) + _tn(kdec_b[h][rows], v_new[h].astype(BF16)) for h in hs]
    for h in hs:
        s_ref[h] = s[h]
        sout_ref[0, h] = s[h]
    o = [cat(inters[h]) + _dot(attn[h], cat(vnews[h]).astype(BF16)) for h in hs]
    for h in hs:
        z = z_ref[:, h * dk:(h + 1) * dk].astype(F32)
        o_ref[:, h * dk:(h + 1) * dk] = (_rms(o[h], gn_ref[...]) * (z * _sigmoid(z))).astype(BF16)

    xbuf[0:8, :] = xbuf[tb:tb + 8, :]


def _gdn(main, small, conv0, ssm0, w_conv, gate_rows, gnorm_w, batch, seq, tb, c, hb):
    n_t = seq // tb
    hh = GDN_HEADS
    dk = GDN_DK
    wd = hb * dk

    def blk(off):
        return pl.BlockSpec((tb, wd), lambda b, h, t: (b * n_t + t, off // hb + h))

    def cblk(off):
        return pl.BlockSpec((1, CONV_W - 1, wd), lambda b, h, t: (b, 0, off // hb + h))

    def wblk(off):
        return pl.BlockSpec((CONV_W, wd), lambda b, h, t: (0, off // hb + h))

    return pl.pallas_call(
        functools.partial(_gdn_kernel, tb=tb, c=c, hb=hb),
        grid=(batch, hh // hb, n_t),
        in_specs=[
            blk(_QB), blk(_KB), blk(_VB), blk(_ZGB),
            pl.BlockSpec((tb, LANE), lambda b, h, t: (b * n_t + t, 1)),
            cblk(_QB), cblk(_KB), cblk(_VB),
            pl.BlockSpec((1, hb, dk, GDN_DV), lambda b, h, t: (b, h, 0, 0)),
            wblk(_QB), wblk(_KB), wblk(_VB),
            pl.BlockSpec((2, LANE), lambda b, h, t: (0, 0)),
            pl.BlockSpec((1, GDN_DV), lambda b, h, t: (0, 0)),
        ],
        out_specs=[
            pl.BlockSpec((tb, wd), lambda b, h, t: (b * n_t + t, h)),
            pl.BlockSpec((1, hb, dk, GDN_DV), lambda b, h, t: (b, h, 0, 0)),
        ],
        out_shape=[
            jax.ShapeDtypeStruct((batch * seq, hh * GDN_DV), BF16),
            jax.ShapeDtypeStruct((batch, hh, dk, GDN_DV), F32),
        ],
        scratch_shapes=[pltpu.VMEM((tb + 8, 3 * wd), F32), pltpu.VMEM((hb, dk, GDN_DV), F32)],
        compiler_params=_params("parallel", "parallel", "arbitrary"),
        name="gdn",
    )(main, main, main, main, small, conv0, conv0, conv0, ssm0, w_conv, w_conv, w_conv, gate_rows, gnorm_w)


def _mla_prep_kernel(cq_ref, ckv_ref, sm_ref, cosq_ref, sinq_ref, cosk_ref, sink_ref, qnw_ref, kvnw_ref,
                     wqt_ref, wuk_ref, wuvt_ref, qt_ref, k_ref, vt_ref, lat_ref, kr_ref):
    scale = QK_DIM ** -0.5 * math.log2(math.e)
    half = ROPE_DIM // 2
    tm = cq_ref.shape[0]
    cqn = _rms(cq_ref[...].astype(F32), qnw_ref[...]).astype(BF16)
    ckvn32 = _rms(ckv_ref[...].astype(F32), kvnw_ref[...])
    lat_ref[0] = ckvn32
    ckvn = ckvn32.astype(BF16)
    sm = sm_ref[...]
    krope32 = sm[:, 0:ROPE_DIM] * cosk_ref[...] + sm[:, ROPE_DIM:2 * ROPE_DIM] * sink_ref[...]
    kr_ref[0] = krope32
    krope = krope32.astype(BF16)
    knope = _dot(ckvn, wuk_ref[...])
    cq_cos = cosq_ref[...]
    cq_sin = sinq_ref[...]
    vt_all = _nt(wuvt_ref[...], ckvn)
    qt_all = _nt(wqt_ref[...], cqn)
    for h in range(MLA_HEADS):
        k_ref[0, h, :, 0:NOPE_DIM] = knope[:, h * NOPE_DIM:(h + 1) * NOPE_DIM].astype(BF16)
        k_ref[0, h, :, NOPE_DIM:QK_DIM] = krope
        vt_ref[0, h, 0:V_DIM, :] = vt_all[h * V_DIM:(h + 1) * V_DIM].astype(BF16)
        vt_ref[0, h, V_DIM:V_ROWS, :] = jnp.ones((V_ROWS - V_DIM, tm), BF16)
        qt = qt_all[h * QK_DIM:(h + 1) * QK_DIM]
        x1 = qt[NOPE_DIM:NOPE_DIM + half]
        x2 = qt[NOPE_DIM + half:QK_DIM]
        qt_ref[0, h, 0:NOPE_DIM, :] = (qt[0:NOPE_DIM] * scale).astype(BF16)
        qt_ref[0, h, NOPE_DIM:NOPE_DIM + half, :] = ((x1 * cq_cos - x2 * cq_sin) * scale).astype(BF16)
        qt_ref[0, h, NOPE_DIM + half:QK_DIM, :] = ((x2 * cq_cos + x1 * cq_sin) * scale).astype(BF16)


def _mla_prep(main, small, cos_t, sin_t, cos2, sin2, qn_w, kvn_w, wq_t, w_uk2, wuv_t, batch, seq, tm):
    n_t = seq // tm
    hh = MLA_HEADS
    half = ROPE_DIM // 2
    const = lambda *shape: pl.BlockSpec(shape, lambda b, i: (0,) * len(shape))
    return pl.pallas_call(
        _mla_prep_kernel,
        grid=(batch, n_t),
        in_specs=[
            pl.BlockSpec((tm, Q_LORA), lambda b, i: (b * n_t + i, _CQ_BLK)),
            pl.BlockSpec((tm, KV_LORA), lambda b, i: (b * n_t + i, _CKV_BLK)),
            pl.BlockSpec((tm, LANE), lambda b, i: (b * n_t + i, 0)),
            pl.BlockSpec((half, tm), lambda b, i: (0, i)),
            pl.BlockSpec((half, tm), lambda b, i: (0, i)),
            pl.BlockSpec((tm, ROPE_DIM), lambda b, i: (i, 0)),
            pl.BlockSpec((tm, ROPE_DIM), lambda b, i: (i, 0)),
            const(1, Q_LORA), const(1, KV_LORA),
            const(hh * QK_DIM, Q_LORA), const(KV_LORA, hh * NOPE_DIM), const(hh * V_DIM, KV_LORA),
        ],
        out_specs=[
            pl.BlockSpec((1, hh, QK_DIM, tm), lambda b, i: (b, 0, 0, i)),
            pl.BlockSpec((1, hh, tm, QK_DIM), lambda b, i: (b, 0, i, 0)),
            pl.BlockSpec((1, hh, V_ROWS, tm), lambda b, i: (b, 0, 0, i)),
            pl.BlockSpec((1, tm, KV_LORA), lambda b, i: (b, i, 0)),
            pl.BlockSpec((1, tm, ROPE_DIM), lambda b, i: (b, i, 0)),
        ],
        out_shape=[
            jax.ShapeDtypeStruct((batch, hh, QK_DIM, seq), BF16),
            jax.ShapeDtypeStruct((batch, hh, seq, QK_DIM), BF16),
            jax.ShapeDtypeStruct((batch, hh, V_ROWS, seq), BF16),
            jax.ShapeDtypeStruct((batch, seq, KV_LORA), F32),
            jax.ShapeDtypeStruct((batch, seq, ROPE_DIM), F32),
        ],
        compiler_params=_params("parallel", "parallel"),
        name="mla_prep",
    )(main, main, small, cos_t, sin_t, cos2, sin2, qn_w, kvn_w, wq_t, w_uk2, wuv_t)


_NEG = -1e30


def _flash_kernel(qt_ref, k_ref, vt_ref, z_ref, bias_ref, o_ref, acc_ref, m_ref, s_buf, p_buf, mb_buf, al_buf,
                  *, tq, tk, unroll):
    r = tq // tk
    strip = min(tk, MXU_COLS)
    i = pl.program_id(2)
    nb = r * (i + 1)

    m_ref[...] = jnp.full(m_ref.shape, _NEG, F32)
    acc_ref[...] = jnp.zeros(acc_ref.shape, F32)
    p_buf[...] = jnp.zeros(p_buf.shape, BF16)
    al_buf[...] = jnp.ones(al_buf.shape, F32)
    s_buf[1] = jnp.full(s_buf.shape[1:], _NEG, F32)
    mb_buf[1] = jnp.full(mb_buf.shape[1:], _NEG, F32)

    def stage_a(blk, slot, u, cols):
        off = pl.multiple_of(blk * tk, tk)
        s = _dot(k_ref[0, 0, pl.ds(off, tk), :], qt_ref[0, 0, :, cols])
        if u >= 0 and cols.start < (u + 1) * tk:
            s = s + bias_ref[:, cols.start - u * tk:cols.stop - u * tk]
        s_buf[slot, :, cols] = s
        mb_buf[slot, :, cols] = jnp.max(s, axis=0, keepdims=True)

    def stage_b(slot, cols):
        m_prev = m_ref[:, cols]
        m_new = jnp.maximum(m_prev, mb_buf[slot, :, cols])
        al_buf[slot, :, cols] = jnp.exp2(m_prev - m_new)
        p_buf[slot, :, cols] = jnp.exp2(s_buf[slot, :, cols] - m_new).astype(BF16)
        m_ref[:, cols] = m_new

    def stage_c(blk, slot, cols):
        off = pl.multiple_of(jnp.maximum(blk, 0) * tk, tk)
        acc_ref[:, cols] = (acc_ref[:, cols] * al_buf[slot, :, cols]
                            + _dot(vt_ref[0, 0, :, pl.ds(off, tk)], p_buf[slot, :, cols]))

    def half_step(c, a, b):
        for c0 in range(0, tq, strip):
            cols = slice(c0, c0 + strip)
            if c is not None and c0 >= max(c[2], 0) * tk:
                stage_c(c[0], c[1], cols)
            if a is not None and c0 >= max(a[2], 0) * tk:
                stage_a(a[0], a[1], a[2], cols)
            if b is not None and c0 >= max(b[1], 0) * tk:
                stage_b(b[0], cols)

    def pair(e, u):
        half_step((e - 2, 0, u - 2), (e, 0, u), (1, u - 1))
        half_step((e - 1, 1, u - 1), (e + 1, 1, u + 1 if u >= 0 else u), (0, u))

    n_pairs = i * (r // 2)
    n_rem = n_pairs % unroll

    def body_rem(jp, carry):
        pair(2 * jp, -4)
        return carry

    def body(jp, carry):
        for x in range(unroll):
            pair(2 * (n_rem + unroll * jp + x), -4)
        return carry

    lax.fori_loop(0, n_rem, body_rem, 0)
    lax.fori_loop(0, n_pairs // unroll, body, 0)
    for x in range(r // 2):
        pair(nb - r + 2 * x, 2 * x)
    half_step((nb - 2, 0, r - 2), None, (1, r - 1))
    half_step((nb - 1, 1, r - 1), None, None)

    o = (acc_ref[0:V_DIM, :] / acc_ref[V_DIM:V_DIM + 1, :]).T
    z = z_ref[...].astype(F32)
    o_ref[...] = (o * (z * _sigmoid(z))).astype(BF16)


def _flash(q_t, k_all, v_t, main, batch, seq, tq, tk):
    n_q = seq // tq
    r = tq // tk
    unroll = 2
    assert r % 2 == 0 and tq % tk == 0
    hh = MLA_HEADS
    shift = int(math.log2(CHUNK))
    cid = jnp.right_shift(jnp.arange(tk, dtype=jnp.int32), shift)
    bias = jnp.where(cid[:, None] <= cid[None, :], 0.0, _NEG).astype(F32)
    return pl.pallas_call(
        functools.partial(_flash_kernel, tq=tq, tk=tk, unroll=unroll),
        grid=(batch, hh, n_q),
        in_specs=[
            pl.BlockSpec((1, 1, QK_DIM, tq), lambda b, h, i: (b, h, 0, i)),
            pl.BlockSpec((1, 1, seq, QK_DIM), lambda b, h, i: (b, h, 0, 0)),
            pl.BlockSpec((1, 1, V_ROWS, seq), lambda b, h, i: (b, h, 0, 0)),
            pl.BlockSpec((tq, V_DIM), lambda b, h, i: (b * n_q + i, _ZMB + h)),
            pl.BlockSpec((tk, tk), lambda b, h, i: (0, 0)),
        ],
        out_specs=pl.BlockSpec((tq, V_DIM), lambda b, h, i: (b * n_q + i, h)),
        out_shape=jax.ShapeDtypeStruct((batch * seq, hh * V_DIM), BF16),
        scratch_shapes=[pltpu.VMEM((V_ROWS, tq), F32), pltpu.VMEM((1, tq), F32),
                        pltpu.VMEM((2, tk, tq), F32), pltpu.VMEM((2, tk, tq), BF16),
                        pltpu.VMEM((2, 1, tq), F32), pltpu.VMEM((2, 1, tq), F32)],
        compiler_params=_params("parallel", "parallel", "arbitrary"),
        name="flash",
    )(q_t, k_all, v_t, main, bias)


def _mla_decode_kernel(cq_ref, ckv_ref, sm_ref, z_ref, plat_ref, pkr_ref, cos_ref, sin_ref, qnw_ref, kvnw_ref,
                       wqn_ref, wqr_ref, wqrr_ref, wukt_ref, wuv_ref, o_ref, lat_ref, kr_ref, *, past, t):
    scale = QK_DIM ** -0.5
    cqn = _rms(cq_ref[...].astype(F32), qnw_ref[...]).astype(BF16)
    ckvn32 = _rms(ckv_ref[...].astype(F32), kvnw_ref[...])
    lat_ref[0] = ckvn32
    ckvn = ckvn32.astype(BF16)
    sm = sm_ref[...]
    cos2 = cos_ref[...]
    sin2 = sin_ref[...]
    krope32 = sm[:, 0:ROPE_DIM] * cos2 + sm[:, ROPE_DIM:2 * ROPE_DIM] * sin2
    kr_ref[0] = krope32
    krope = krope32.astype(BF16)

    qlats, qropes = [], []
    for h in range(MLA_HEADS):
        qn = _dot(cqn, wqn_ref[h]).astype(BF16)
        qlats.append((_dot(qn, wukt_ref[h]) * scale).astype(BF16))
        qr = _dot(cqn, wqr_ref[h]) * cos2 + _dot(cqn, wqrr_ref[h]) * sin2
        qropes.append((qr * scale).astype(BF16))
    qlat = jnp.concatenate(qlats, axis=0)
    qrope = jnp.concatenate(qropes, axis=0)

    plat = plat_ref[0].astype(BF16)
    pkr = pkr_ref[0].astype(BF16)
    s_past = _nt(qlat, plat) + _nt(qrope, pkr)
    s_new = _nt(qlat, ckvn) + _nt(qrope, krope)
    rows = MLA_HEADS * t
    qpos = past + jnp.bitwise_and(lax.broadcasted_iota(jnp.int32, (rows, t), 0), t - 1)
    kpos = past + lax.broadcasted_iota(jnp.int32, (rows, t), 1)
    shift = int(math.log2(CHUNK))
    visible = kpos < jnp.left_shift(jnp.right_shift(qpos, shift) + 1, shift)
    s_new = jnp.where(visible, s_new, -jnp.inf)
    m = jnp.maximum(jnp.max(s_past, axis=-1, keepdims=True), jnp.max(s_new, axis=-1, keepdims=True))
    p_past = jnp.exp(s_past - m)
    p_new = jnp.exp(s_new - m)
    l = jnp.sum(p_past, axis=-1, keepdims=True) + jnp.sum(p_new, axis=-1, keepdims=True)
    o_lat = (_dot(p_past.astype(BF16), plat) + _dot(p_new.astype(BF16), ckvn)) / l
    z = z_ref[...].astype(F32)
    gate = z * _sigmoid(z)
    for h in range(MLA_HEADS):
        o_h = _dot(o_lat[h * t:(h + 1) * t].astype(BF16), wuv_ref[h])
        cols = slice(h * V_DIM, (h + 1) * V_DIM)
        o_ref[:, cols] = (o_h * gate[:, cols]).astype(BF16)


def _mla_decode(main, small, past_lat, past_kr, cos2, sin2, qn_w, kvn_w, wq_n, wq_r, wq_rr, wuk_t, wuv, batch, t):
    past = past_lat.shape[1]
    hh = MLA_HEADS
    const = lambda *shape: pl.BlockSpec(shape, lambda b: (0,) * len(shape))
    return pl.pallas_call(
        functools.partial(_mla_decode_kernel, past=past, t=t),
        grid=(batch,),
        in_specs=[
            pl.BlockSpec((t, Q_LORA), lambda b: (b, _CQ_BLK)),
            pl.BlockSpec((t, KV_LORA), lambda b: (b, _CKV_BLK)),
            pl.BlockSpec((t, LANE), lambda b: (b, 0)),
            pl.BlockSpec((t, hh * V_DIM), lambda b: (b, _ZMB // hh)),
            pl.BlockSpec((1, past, KV_LORA), lambda b: (b, 0, 0)),
            pl.BlockSpec((1, past, ROPE_DIM), lambda b: (b, 0, 0)),
            const(t, ROPE_DIM), const(t, ROPE_DIM),
            const(1, Q_LORA), const(1, KV_LORA),
            const(hh, Q_LORA, NOPE_DIM), const(hh, Q_LORA, ROPE_DIM), const(hh, Q_LORA, ROPE_DIM),
            const(hh, NOPE_DIM, KV_LORA), const(hh, KV_LORA, V_DIM),
        ],
        out_specs=[
            pl.BlockSpec((t, hh * V_DIM), lambda b: (b, 0)),
            pl.BlockSpec((1, t, KV_LORA), lambda b: (b, 0, 0)),
            pl.BlockSpec((1, t, ROPE_DIM), lambda b: (b, 0, 0)),
        ],
        out_shape=[
            jax.ShapeDtypeStruct((batch * t, hh * V_DIM), BF16),
            jax.ShapeDtypeStruct((batch, t, KV_LORA), F32),
            jax.ShapeDtypeStruct((batch, t, ROPE_DIM), F32),
        ],
        compiler_params=_params("parallel"),
        name="mla_decode",
    )(main, main, small, main, past_lat, past_kr, cos2, sin2, qn_w, kvn_w, wq_n, wq_r, wq_rr, wuk_t, wuv)


def _rope_tables(p0, t):
    pos = jnp.arange(p0, p0 + t, dtype=jnp.int32)
    inv = ROPE_THETA ** (-jnp.arange(0, ROPE_DIM, 2, dtype=F32) / ROPE_DIM)
    ang = pos.astype(F32)[:, None] * inv[None, :]
    return jnp.cos(ang), jnp.sin(ang)


def _pick(n, prefs):
    for p in prefs:
        if n % p == 0:
            return p
    return n


def _layer(x, conv0, ssm0, past_lat, past_kr, lw, lnf_w):
    (ln_w, w_in, w_conv, a_log, dt_bias, gn_w, qn_w, w_q_up, kvn_w, w_uk, w_uv, w_out) = lw
    batch, t, d = x.shape
    n = batch * t
    hh = MLA_HEADS
    half = ROPE_DIM // 2
    x2d = x.reshape(n, d)

    sizes = (3 * GDN_HEADS * GDN_DK, GDN_HEADS * GDN_DV, GDN_HEADS, GDN_HEADS, Q_LORA, KV_LORA, ROPE_DIM, hh * V_DIM)
    offs = [0]
    for s in sizes:
        offs.append(offs[-1] + s)
    w_qkv, w_zg, w_bg, w_ag, w_cq, w_ckv, w_kr, w_zm = (w_in[:, offs[i]:offs[i + 1]] for i in range(8))
    w_main = jnp.concatenate([w_qkv, w_zg, w_zm, w_cq, w_ckv], axis=1).astype(BF16)
    w_small = jnp.concatenate(
        [w_kr, w_kr[:, half:], w_kr[:, :half], w_bg, w_ag,
         jnp.zeros((d, _SMALL_COLS - 2 * ROPE_DIM - 2 * GDN_HEADS), F32)], axis=1).astype(BF16)
    pad = jnp.zeros((LANE - 2 * GDN_HEADS,), F32)
    gate_rows = jnp.stack([jnp.concatenate([jnp.zeros((GDN_HEADS,), F32), a_log, pad]),
                           jnp.concatenate([jnp.zeros((GDN_HEADS,), F32), dt_bias, pad])])

    tm = _pick(n, (1024, 512, 256))
    main, small = _in_proj(x2d, ln_w.reshape(1, d), w_main, w_small, tm, 2048)

    c = min(GDN_CHUNK, t)
    tb = _pick(t, (256, 128, 64))
    o_g, new_ssm = _gdn(main, small, conv0, ssm0, w_conv, gate_rows, gn_w.reshape(1, GDN_DV), batch, t, tb, c, 8)
    qkv_rows = main.reshape(batch, t, -1)[:, max(t - (CONV_W - 1), 0):, :3 * GDN_HEADS * GDN_DK].astype(F32)
    new_conv = jnp.concatenate([conv0, qkv_rows], axis=1)[:, -(CONV_W - 1):]

    p_len = past_lat.shape[1]
    cos, sin = _rope_tables(p_len, t)
    cos2 = jnp.concatenate([cos, cos], axis=1)
    sin2 = jnp.concatenate([-sin, sin], axis=1)
    qn_w2 = qn_w.reshape(1, Q_LORA)
    kvn_w2 = kvn_w.reshape(1, KV_LORA)
    if p_len == 0:
        wq_t = jnp.transpose(w_q_up, (1, 2, 0)).reshape(hh * QK_DIM, Q_LORA).astype(BF16)
        w_uk2 = w_uk.reshape(KV_LORA, hh * NOPE_DIM).astype(BF16)
        wuv_t = jnp.transpose(w_uv, (1, 2, 0)).reshape(hh * V_DIM, KV_LORA).astype(BF16)
        tmq = _pick(t, (512, 256, 128))
        q_t, k_all, v_t, lat, kr = _mla_prep(main, small, cos.T, sin.T, cos2, sin2, qn_w2, kvn_w2,
                                             wq_t, w_uk2, wuv_t, batch, t, tmq)
        tq = _pick(t, (1024, 512, 256))
        o_m = _flash(q_t, k_all, v_t, main, batch, t, tq, tq // 4)
    else:
        wq_n = jnp.transpose(w_q_up[:, :, :NOPE_DIM], (1, 0, 2)).astype(BF16)
        wq_r = jnp.transpose(w_q_up[:, :, NOPE_DIM:], (1, 0, 2))
        wq_rr = jnp.concatenate([wq_r[..., half:], wq_r[..., :half]], axis=-1).astype(BF16)
        wuk_t = jnp.transpose(w_uk, (1, 2, 0)).astype(BF16)
        wuv = jnp.transpose(w_uv, (1, 0, 2)).astype(BF16)
        o_m, lat, kr = _mla_decode(main, small, past_lat, past_kr, cos2, sin2, qn_w2, kvn_w2,
                                   wq_n, wq_r.astype(BF16), wq_rr, wuk_t, wuv, batch, t)

    w_g = w_out[:GDN_HEADS * GDN_DV].astype(BF16)
    w_m = w_out[GDN_HEADS * GDN_DV:].astype(BF16)
    y = _out_proj(x2d, o_g, o_m, w_g, w_m, lnf_w.reshape(1, d), _pick(n, (512, 256)))
    return y.reshape(batch, t, d), new_conv, new_ssm, lat, kr


def kernel(x_prompt, x_sample, cache_mla_latent, cache_mla_krope, state_gdn_conv, state_gdn_ssm, ln_in_w, w_in, w_conv, a_log, dt_bias, gdn_norm_w, q_norm_w, w_q_up, kv_norm_w, w_uk, w_uv, w_out, ln_final_w):
    depth = w_in.shape[0]
    assert depth == 1, "the final norm is fused into the out-projection of the single layer"
    bp = x_prompt.shape[0]
    lw = (ln_in_w[0], w_in[0], w_conv[0], a_log[0], dt_bias[0], gdn_norm_w[0],
          q_norm_w[0], w_q_up[0], kv_norm_w[0], w_uk[0], w_uv[0], w_out[0])
    zc = jnp.zeros((bp, CONV_W - 1, 3 * GDN_HEADS * GDN_DK), F32)
    zs = jnp.zeros((bp, GDN_HEADS, GDN_DK, GDN_DV), F32)
    yp, c1, s1, l1, r1 = _layer(x_prompt, zc, zs, jnp.zeros((bp, 0, KV_LORA), F32),
                                jnp.zeros((bp, 0, ROPE_DIM), F32), lw, ln_final_w)
    ys, c2, s2, l2, r2 = _layer(x_sample, state_gdn_conv[0], state_gdn_ssm[0], cache_mla_latent[0],
                                cache_mla_krope[0], lw, ln_final_w)
    return (yp, ys, c1[None], s1[None], l1[None], r1[None], c2[None], s2[None], l2[None], r2[None])
```

```python
import functools
import math

import jax
import jax.numpy as jnp
from jax import lax
from jax.experimental import pallas as pl
from jax.experimental.pallas import tpu as pltpu

F32 = jnp.float32
BF16 = jnp.bfloat16

EPS = 1e-6
CHUNK = 64
GDN_CHUNK = 64
GDN_HEADS = 8
GDN_DK = 128
GDN_DV = 128
CONV_W = 4
MLA_HEADS = 8
Q_LORA = 512
KV_LORA = 512
NOPE_DIM = 128
ROPE_DIM = 64
V_DIM = 128
ROPE_THETA = 10000.0
QK_DIM = NOPE_DIM + ROPE_DIM
V_ROWS = V_DIM + 16

LANE = 128
MXU_COLS = 256
VMEM_LIMIT = 56 * 1024 * 1024

_QB, _KB, _VB, _ZGB, _ZMB = 0, 8, 16, 24, 32
_CQ_BLK, _CKV_BLK = 10, 11
_MAIN_COLS = 6144
_SMALL_COLS = 256


def _nt(a, b):
    return lax.dot_general(a, b, (((1,), (1,)), ((), ())), preferred_element_type=F32)


def _tn(a, b):
    return lax.dot_general(a, b, (((0,), (0,)), ((), ())), preferred_element_type=F32)


def _dot(a, b):
    return jnp.dot(a, b, preferred_element_type=F32)


def _rms(x, w):
    return x * lax.rsqrt(jnp.mean(x * x, axis=-1, keepdims=True) + EPS) * w


def _sigmoid(x):
    return 1.0 / (1.0 + jnp.exp(-x))


def _params(*sem):
    return pltpu.CompilerParams(dimension_semantics=sem, vmem_limit_bytes=VMEM_LIMIT)


def _in_proj_kernel(x_ref, lnw_ref, wm_ref, ws_ref, main_ref, small_ref, h_ref, *, rows):
    @pl.when(pl.program_id(1) == 0)
    def _():
        def body(r, c):
            sl = pl.ds(pl.multiple_of(r * rows, rows), rows)
            h_ref[sl, :] = _rms(x_ref[sl, :], lnw_ref[...]).astype(BF16)
            return c
        lax.fori_loop(0, x_ref.shape[0] // rows, body, 0)
        small_ref[...] = _dot(h_ref[...], ws_ref[...])

    main_ref[...] = _dot(h_ref[...], wm_ref[...]).astype(BF16)


def _in_proj(x2d, ln_w, w_main, w_small, tm, tn):
    n, d = x2d.shape
    nm, ns = w_main.shape[1], w_small.shape[1]
    rows = min(tm, 128)
    return pl.pallas_call(
        functools.partial(_in_proj_kernel, rows=rows),
        grid=(n // tm, nm // tn),
        in_specs=[
            pl.BlockSpec((tm, d), lambda i, j: (i, 0)),
            pl.BlockSpec((1, d), lambda i, j: (0, 0)),
            pl.BlockSpec((d, tn), lambda i, j: (0, j)),
            pl.BlockSpec((d, ns), lambda i, j: (0, 0)),
        ],
        out_specs=[
            pl.BlockSpec((tm, tn), lambda i, j: (i, j)),
            pl.BlockSpec((tm, ns), lambda i, j: (i, 0)),
        ],
        out_shape=[
            jax.ShapeDtypeStruct((n, nm), BF16),
            jax.ShapeDtypeStruct((n, ns), F32),
        ],
        scratch_shapes=[pltpu.VMEM((tm, d), BF16)],
        compiler_params=_params("parallel", "arbitrary"),
        name="in_proj",
    )(x2d, ln_w, w_main, w_small)


def _out_proj_kernel(x_ref, og_ref, om_ref, wg_ref, wm_ref, lnf_ref, y_ref):
    acc = _dot(og_ref[...], wg_ref[...]) + _dot(om_ref[...], wm_ref[...])
    y_ref[...] = _rms(x_ref[...] + acc, lnf_ref[...])


def _out_proj(x2d, o_g, o_m, w_g, w_m, lnf_w, tm):
    n, d = x2d.shape
    kg, km = o_g.shape[1], o_m.shape[1]
    return pl.pallas_call(
        _out_proj_kernel,
        grid=(n // tm,),
        in_specs=[
            pl.BlockSpec((tm, d), lambda i: (i, 0)),
            pl.BlockSpec((tm, kg), lambda i: (i, 0)),
            pl.BlockSpec((tm, km), lambda i: (i, 0)),
            pl.BlockSpec((kg, d), lambda i: (0, 0)),
            pl.BlockSpec((km, d), lambda i: (0, 0)),
            pl.BlockSpec((1, d), lambda i: (0, 0)),
        ],
        out_specs=pl.BlockSpec((tm, d), lambda i: (i, 0)),
        out_shape=jax.ShapeDtypeStruct((n, d), F32),
        compiler_params=_params("parallel"),
        name="out_proj",
    )(x2d, o_g, o_m, w_g, w_m, lnf_w)


def _gdn_kernel(q_ref, k_ref, v_ref, z_ref, sm_ref, cq0_ref, ck0_ref, cv0_ref, s0_ref,
                wq_ref, wk_ref, wv_ref, gp_ref, gn_ref, o_ref, sout_ref, xbuf, s_ref, *, tb, c, hb):
    hg = pl.program_id(1)
    t = pl.program_id(2)
    dk = GDN_DK
    wd = hb * dk

    @pl.when(t == 0)
    def _():
        xbuf[0:8, :] = jnp.zeros((8, 3 * wd), F32)
        xbuf[8 - (CONV_W - 1):8, 0:wd] = cq0_ref[0]
        xbuf[8 - (CONV_W - 1):8, wd:2 * wd] = ck0_ref[0]
        xbuf[8 - (CONV_W - 1):8, 2 * wd:3 * wd] = cv0_ref[0]
        s_ref[...] = s0_ref[0]

    xbuf[8:8 + tb, 0:wd] = q_ref[...].astype(F32)
    xbuf[8:8 + tb, wd:2 * wd] = k_ref[...].astype(F32)
    xbuf[8:8 + tb, 2 * wd:3 * wd] = v_ref[...].astype(F32)

    srow = lax.broadcasted_iota(jnp.int32, (tb, tb), 0)
    scol = lax.broadcasted_iota(jnp.int32, (tb, tb), 1)
    shifts = [jnp.where(scol == srow - i, 1.0, 0.0).astype(BF16) for i in range(1, CONV_W)]
    gw = 2 if hb % 2 == 0 else 1
    row8 = lax.broadcasted_iota(jnp.int32, (8, gw * dk), 0)

    def conv(idx, x_ref, w_ref, hp):
        lo = hp * gw * dk
        w = w_ref[:, lo:lo + gw * dk]
        cols = slice(idx * wd + lo, idx * wd + lo + gw * dk)
        y = xbuf[8:8 + tb, cols] * w[CONV_W - 1:CONV_W]
        head = jnp.zeros((8, gw * dk), F32)
        for i in range(1, CONV_W):
            wi = w[CONV_W - 1 - i:CONV_W - i]
            y = y + _dot(shifts[i - 1], x_ref[:, lo:lo + gw * dk]) * wi
            head = head + jnp.where(row8 < i, xbuf[8 - i:16 - i, cols], 0.0) * wi
        y = jnp.concatenate([y[0:8] + head, y[8:]], axis=0)
        return y * _sigmoid(y)

    sm = sm_ref[...]
    gp = gp_ref[...]
    beta_all = _sigmoid(sm)
    xg = sm + gp[1:2]
    softplus = jnp.maximum(xg, 0.0) + jnp.log(1.0 + jnp.exp(-jnp.abs(xg)))
    g_all = -jnp.exp(gp[0:1]) * softplus
    lane = lax.broadcasted_iota(jnp.int32, (tb, LANE), 1)

    shift = int(math.log2(c))
    nc = tb // c
    row = lax.broadcasted_iota(jnp.int32, (tb, tb), 0)
    col = lax.broadcasted_iota(jnp.int32, (tb, tb), 1)
    same = jnp.right_shift(row, shift) == jnp.right_shift(col, shift)
    lincl = jnp.where(jnp.logical_and(same, col <= row), 1.0, 0.0).astype(BF16)
    bd_mask = jnp.where(same, 1.0, 0.0).astype(BF16)
    rl = lax.broadcasted_iota(jnp.int32, (c, tb), 0)
    ll = lax.broadcasted_iota(jnp.int32, (c, tb), 1)
    jl = jnp.bitwise_and(ll, c - 1)
    lb = jnp.right_shift(ll, shift)
    incl = jl <= rl
    strict = jl < rl
    eye = jnp.where(jl == rl, 1.0, 0.0)

    def to_ls(x):
        out = x[(nc - 1) * c:nc * c]
        for n in range(nc - 2, -1, -1):
            out = jnp.where(lb == n, x[n * c:(n + 1) * c], out)
        return out

    def to_bd(x_b):
        if nc == 1:
            return x_b
        return jnp.concatenate([x_b] * nc, axis=0) * bd_mask

    def widen(x):
        return x[:, :tb] if tb <= dk else jnp.concatenate([x] * (tb // dk), axis=1)

    hs = range(hb)
    beta = [jnp.sum(jnp.where(lane == hg * hb + h, beta_all, 0.0), axis=-1, keepdims=True) for h in hs]
    g = [jnp.sum(jnp.where(lane == hg * hb + h + GDN_HEADS, g_all, 0.0), axis=-1, keepdims=True) for h in hs]

    g_hi = [x.astype(BF16).astype(F32) for x in g]
    g_lo = [(x - y).astype(BF16).astype(F32) for x, y in zip(g, g_hi)]
    gsum = [_dot(lincl, jnp.concatenate([jnp.broadcast_to(x, (tb, dk)), jnp.broadcast_to(y, (tb, dk))],
                                        axis=1).astype(BF16)) for x, y in zip(g_hi, g_lo)]
    gcb = [x[:, :dk] + x[:, dk:] for x in gsum]
    gc_row = [x.T[0:1, :] for x in gcb]
    decay = [jnp.where(incl, jnp.exp(jnp.where(incl, to_ls(widen(x)) - y, 0.0)), 0.0)
             for x, y in zip(gcb, gc_row)]
    egc = [jnp.exp(x) for x in gcb]
    glast = [[x[(ci + 1) * c - 1:(ci + 1) * c, :] for ci in range(nc)] for x in gcb]
    cat = lambda xs: xs[0] if len(xs) == 1 else jnp.concatenate(xs, axis=0)
    kfac = [cat([jnp.exp(glast[h][ci] - gcb[h][ci * c:(ci + 1) * c]) for ci in range(nc)]) for h in hs]

    k_b, kb_b, kbg_b, q_b, qg_b, vb_b, kdec_b = ([None] * hb for _ in range(7))
    for hp in range(hb // gw):
        q2 = conv(0, q_ref, wq_ref, hp)
        k2 = conv(1, k_ref, wk_ref, hp)
        v2 = conv(2, v_ref, wv_ref, hp)
        for j in range(gw):
            h = hp * gw + j
            sl = slice(j * dk, (j + 1) * dk)
            qh, kh, vh = q2[:, sl], k2[:, sl], v2[:, sl]
            qh = qh * (lax.rsqrt(jnp.sum(qh * qh, axis=-1, keepdims=True) + EPS) * (dk ** -0.5))
            kh = kh * lax.rsqrt(jnp.sum(kh * kh, axis=-1, keepdims=True) + EPS)
            kbh = kh * beta[h]
            k_b[h] = kh.astype(BF16)
            kb_b[h] = kbh.astype(BF16)
            kbg_b[h] = (kbh * egc[h]).astype(BF16)
            q_b[h] = qh.astype(BF16)
            qg_b[h] = (qh * egc[h]).astype(BF16)
            vb_b[h] = (vh * beta[h]).astype(BF16)
            kdec_b[h] = (kh * kfac[h]).astype(BF16)

    kk = [to_ls(_nt(x, y)) for x, y in zip(kb_b, k_b)]
    p = [jnp.where(strict, -(x * y), 0.0) for x, y in zip(kk, decay)]
    tmat = [eye + x for x in p]
    p = [_dot(x.astype(BF16), to_bd(x.astype(BF16))) for x in p]
    for lvl in range(1, shift):
        p_b = [x.astype(BF16) for x in p]
        p_bd = [to_bd(x) for x in p_b]
        if lvl < shift - 1:
            both = [_dot(jnp.concatenate([x, y.astype(BF16)], axis=0), z) for x, y, z in zip(p_b, tmat, p_bd)]
            p = [x[:c] for x in both]
            tmat = [x + y[c:] for x, y in zip(tmat, both)]
        else:
            tmat = [x + _dot(x.astype(BF16), y) for x, y in zip(tmat, p_bd)]
    t_bd = [to_bd(x.astype(BF16)) for x in tmat]
    uw = [_dot(x, jnp.concatenate([y, z], axis=1)) for x, y, z in zip(t_bd, vb_b, kbg_b)]
    u = [x[:, :GDN_DV] for x in uw]
    w_b = [x[:, GDN_DV:].astype(BF16) for x in uw]
    attn = [to_bd(jnp.where(incl, to_ls(_nt(x, y)) * z, 0.0).astype(BF16))
            for x, y, z in zip(q_b, k_b, decay)]

    s = [s_ref[h] for h in hs]
    vnews = [[] for _ in hs]
    inters = [[] for _ in hs]
    for ci in range(tb // c):
        rows = slice(ci * c, (ci + 1) * c)
        s_b = [x.astype(BF16) for x in s]
        both = [_dot(jnp.concatenate([w_b[h][rows], qg_b[h][rows]], axis=0), s_b[h]) for h in hs]
        v_new = [u[h][rows] - both[h][:c] for h in hs]
        for h in hs:
            inters[h].append(both[h][c:])
            vnews[h].append(v_new[h])
        s = [s[h] * jnp.exp(glast[h][ci]) + _tn(kdec_b[h][rows], v_new[h].astype(BF16)) for h in hs]
    for h in hs:
        s_ref[h] = s[h]
        sout_ref[0, h] = s[h]
    o = [cat(inters[h]) + _dot(attn[h], cat(vnews[h]).astype(BF16)) for h in hs]
    for h in hs:
        z = z_ref[:, h * dk:(h + 1) * dk].astype(F32)
        o_ref[:, h * dk:(h + 1) * dk] = (_rms(o[h], gn_ref[...]) * (z * _sigmoid(z))).astype(BF16)

    xbuf[0:8, :] = xbuf[tb:tb + 8, :]


def _gdn(main, small, conv0, ssm0, w_conv, gate_rows, gnorm_w, batch, seq, tb, c, hb):
    n_t = seq // tb
    hh = GDN_HEADS
    dk = GDN_DK
    wd = hb * dk

    def blk(off):
        return pl.BlockSpec((tb, wd), lambda b, h, t: (b * n_t + t, off // hb + h))

    def cblk(off):
        return pl.BlockSpec((1, CONV_W - 1, wd), lambda b, h, t: (b, 0, off // hb + h))

    def wblk(off):
        return pl.BlockSpec((CONV_W, wd), lambda b, h, t: (0, off // hb + h))

    return pl.pallas_call(
        functools.partial(_gdn_kernel, tb=tb, c=c, hb=hb),
        grid=(batch, hh // hb, n_t),
        in_specs=[
            blk(_QB), blk(_KB), blk(_VB), blk(_ZGB),
            pl.BlockSpec((tb, LANE), lambda b, h, t: (b * n_t + t, 1)),
            cblk(_QB), cblk(_KB), cblk(_VB),
            pl.BlockSpec((1, hb, dk, GDN_DV), lambda b, h, t: (b, h, 0, 0)),
            wblk(_QB), wblk(_KB), wblk(_VB),
            pl.BlockSpec((2, LANE), lambda b, h, t: (0, 0)),
            pl.BlockSpec((1, GDN_DV), lambda b, h, t: (0, 0)),
        ],
        out_specs=[
            pl.BlockSpec((tb, wd), lambda b, h, t: (b * n_t + t, h)),
            pl.BlockSpec((1, hb, dk, GDN_DV), lambda b, h, t: (b, h, 0, 0)),
        ],
        out_shape=[
            jax.ShapeDtypeStruct((batch * seq, hh * GDN_DV), BF16),
            jax.ShapeDtypeStruct((batch, hh, dk, GDN_DV), F32),
        ],
        scratch_shapes=[pltpu.VMEM((tb + 8, 3 * wd), F32), pltpu.VMEM((hb, dk, GDN_DV), F32)],
        compiler_params=_params("parallel", "parallel", "arbitrary"),
        name="gdn",
    )(main, main, main, main, small, conv0, conv0, conv0, ssm0, w_conv, w_conv, w_conv, gate_rows, gnorm_w)


def _mla_prep_kernel(cq_ref, ckv_ref, sm_ref, cosq_ref, sinq_ref, cosk_ref, sink_ref, qnw_ref, kvnw_ref,
                     wqt_ref, wuk_ref, wuvt_ref, qt_ref, k_ref, vt_ref, lat_ref, kr_ref):
    scale = QK_DIM ** -0.5 * math.log2(math.e)
    half = ROPE_DIM // 2
    tm = cq_ref.shape[0]
    cqn = _rms(cq_ref[...].astype(F32), qnw_ref[...]).astype(BF16)
    ckvn32 = _rms(ckv_ref[...].astype(F32), kvnw_ref[...])
    lat_ref[0] = ckvn32
    ckvn = ckvn32.astype(BF16)
    sm = sm_ref[...]
    krope32 = sm[:, 0:ROPE_DIM] * cosk_ref[...] + sm[:, ROPE_DIM:2 * ROPE_DIM] * sink_ref[...]
    kr_ref[0] = krope32
    krope = krope32.astype(BF16)
    knope = _dot(ckvn, wuk_ref[...])
    cq_cos = cosq_ref[...]
    cq_sin = sinq_ref[...]
    vt_all = _nt(wuvt_ref[...], ckvn)
    qt_all = _nt(wqt_ref[...], cqn)
    for h in range(MLA_HEADS):
        k_ref[0, h, :, 0:NOPE_DIM] = knope[:, h * NOPE_DIM:(h + 1) * NOPE_DIM].astype(BF16)
        k_ref[0, h, :, NOPE_DIM:QK_DIM] = krope
        vt_ref[0, h, 0:V_DIM, :] = vt_all[h * V_DIM:(h + 1) * V_DIM].astype(BF16)
        vt_ref[0, h, V_DIM:V_ROWS, :] = jnp.ones((V_ROWS - V_DIM, tm), BF16)
        qt = qt_all[h * QK_DIM:(h + 1) * QK_DIM]
        x1 = qt[NOPE_DIM:NOPE_DIM + half]
        x2 = qt[NOPE_DIM + half:QK_DIM]
        qt_ref[0, h, 0:NOPE_DIM, :] = (qt[0:NOPE_DIM] * scale).astype(BF16)
        qt_ref[0, h, NOPE_DIM:NOPE_DIM + half, :] = ((x1 * cq_cos - x2 * cq_sin) * scale).astype(BF16)
        qt_ref[0, h, NOPE_DIM + half:QK_DIM, :] = ((x2 * cq_cos + x1 * cq_sin) * scale).astype(BF16)


def _mla_prep(main, small, cos_t, sin_t, cos2, sin2, qn_w, kvn_w, wq_t, w_uk2, wuv_t, batch, seq, tm):
    n_t = seq // tm
    hh = MLA_HEADS
    half = ROPE_DIM // 2
    const = lambda *shape: pl.BlockSpec(shape, lambda b, i: (0,) * len(shape))
    return pl.pallas_call(
        _mla_prep_kernel,
        grid=(batch, n_t),
        in_specs=[
            pl.BlockSpec((tm, Q_LORA), lambda b, i: (b * n_t + i, _CQ_BLK)),
            pl.BlockSpec((tm, KV_LORA), lambda b, i: (b * n_t + i, _CKV_BLK)),
            pl.BlockSpec((tm, LANE), lambda b, i: (b * n_t + i, 0)),
            pl.BlockSpec((half, tm), lambda b, i: (0, i)),
            pl.BlockSpec((half, tm), lambda b, i: (0, i)),
            pl.BlockSpec((tm, ROPE_DIM), lambda b, i: (i, 0)),
            pl.BlockSpec((tm, ROPE_DIM), lambda b, i: (i, 0)),
            const(1, Q_LORA), const(1, KV_LORA),
            const(hh * QK_DIM, Q_LORA), const(KV_LORA, hh * NOPE_DIM), const(hh * V_DIM, KV_LORA),
        ],
        out_specs=[
            pl.BlockSpec((1, hh, QK_DIM, tm), lambda b, i: (b, 0, 0, i)),
            pl.BlockSpec((1, hh, tm, QK_DIM), lambda b, i: (b, 0, i, 0)),
            pl.BlockSpec((1, hh, V_ROWS, tm), lambda b, i: (b, 0, 0, i)),
            pl.BlockSpec((1, tm, KV_LORA), lambda b, i: (b, i, 0)),
            pl.BlockSpec((1, tm, ROPE_DIM), lambda b, i: (b, i, 0)),
        ],
        out_shape=[
            jax.ShapeDtypeStruct((batch, hh, QK_DIM, seq), BF16),
            jax.ShapeDtypeStruct((batch, hh, seq, QK_DIM), BF16),
            jax.ShapeDtypeStruct((batch, hh, V_ROWS, seq), BF16),
            jax.ShapeDtypeStruct((batch, seq, KV_LORA), F32),
            jax.ShapeDtypeStruct((batch, seq, ROPE_DIM), F32),
        ],
        compiler_params=_params("parallel", "parallel"),
        name="mla_prep",
    )(main, main, small, cos_t, sin_t, cos2, sin2, qn_w, kvn_w, wq_t, w_uk2, wuv_t)


_NEG = -1e30


def _flash_kernel(qt_ref, k_ref, vt_ref, z_ref, bias_ref, o_ref, acc_ref, m_ref, s_buf, p_buf, mb_buf, al_buf,
                  *, tq, tk, unroll):
    r = tq // tk
    strip = min(tk, MXU_COLS)
    i = pl.program_id(2)
    nb = r * (i + 1)

    m_ref[...] = jnp.full(m_ref.shape, _NEG, F32)
    acc_ref[...] = jnp.zeros(acc_ref.shape, F32)
    p_buf[...] = jnp.zeros(p_buf.shape, BF16)
    al_buf[...] = jnp.ones(al_buf.shape, F32)
    s_buf[1] = jnp.full(s_buf.shape[1:], _NEG, F32)
    mb_buf[1] = jnp.full(mb_buf.shape[1:], _NEG, F32)

    def stage_a(blk, slot, u, cols):
        off = pl.multiple_of(blk * tk, tk)
        s = _dot(k_ref[0, 0, pl.ds(off, tk), :], qt_ref[0, 0, :, cols])
        if u >= 0 and cols.start < (u + 1) * tk:
            s = s + bias_ref[:, cols.start - u * tk:cols.stop - u * tk]
        s_buf[slot, :, cols] = s
        mb_buf[slot, :, cols] = jnp.max(s, axis=0, keepdims=True)

    def stage_b(slot, cols):
        m_prev = m_ref[:, cols]
        m_new = jnp.maximum(m_prev, mb_buf[slot, :, cols])
        al_buf[slot, :, cols] = jnp.exp2(m_prev - m_new)
        p_buf[slot, :, cols] = jnp.exp2(s_buf[slot, :, cols] - m_new).astype(BF16)
        m_ref[:, cols] = m_new

    def stage_c(blk, slot, cols):
        off = pl.multiple_of(jnp.maximum(blk, 0) * tk, tk)
        acc_ref[:, cols] = (acc_ref[:, cols] * al_buf[slot, :, cols]
                            + _dot(vt_ref[0, 0, :, pl.ds(off, tk)], p_buf[slot, :, cols]))

    def half_step(c, a, b):
        for c0 in range(0, tq, strip):
            cols = slice(c0, c0 + strip)
            if c is not None and c0 >= max(c[2], 0) * tk:
                stage_c(c[0], c[1], cols)
            if a is not None and c0 >= max(a[2], 0) * tk:
                stage_a(a[0], a[1], a[2], cols)
            if b is not None and c0 >= max(b[1], 0) * tk:
                stage_b(b[0], cols)

    def pair(e, u):
        half_step((e - 2, 0, u - 2), (e, 0, u), (1, u - 1))
        half_step((e - 1, 1, u - 1), (e + 1, 1, u + 1 if u >= 0 else u), (0, u))

    n_pairs = i * (r // 2)
    n_rem = n_pairs % unroll

    def body_rem(jp, carry):
        pair(2 * jp, -4)
        return carry

    def body(jp, carry):
        for x in range(unroll):
            pair(2 * (n_rem + unroll * jp + x), -4)
        return carry

    lax.fori_loop(0, n_rem, body_rem, 0)
    lax.fori_loop(0, n_pairs // unroll, body, 0)
    for x in range(r // 2):
        pair(nb - r + 2 * x, 2 * x)
    half_step((nb - 2, 0, r - 2), None, (1, r - 1))
    half_step((nb - 1, 1, r - 1), None, None)

    o = (acc_ref[0:V_DIM, :] / acc_ref[V_DIM:V_DIM + 1, :]).T
    z = z_ref[...].astype(F32)
    o_ref[...] = (o * (z * _sigmoid(z))).astype(BF16)


def _flash(q_t, k_all, v_t, main, batch, seq, tq, tk):
    n_q = seq // tq
    r = tq // tk
    unroll = 2
    assert r % 2 == 0 and tq % tk == 0
    hh = MLA_HEADS
    shift = int(math.log2(CHUNK))
    cid = jnp.right_shift(jnp.arange(tk, dtype=jnp.int32), shift)
    bias = jnp.where(cid[:, None] <= cid[None, :], 0.0, _NEG).astype(F32)
    return pl.pallas_call(
        functools.partial(_flash_kernel, tq=tq, tk=tk, unroll=unroll),
        grid=(batch, hh, n_q),
        in_specs=[
            pl.BlockSpec((1, 1, QK_DIM, tq), lambda b, h, i: (b, h, 0, i)),
            pl.BlockSpec((1, 1, seq, QK_DIM), lambda b, h, i: (b, h, 0, 0)),
            pl.BlockSpec((1, 1, V_ROWS, seq), lambda b, h, i: (b, h, 0, 0)),
            pl.BlockSpec((tq, V_DIM), lambda b, h, i: (b * n_q + i, _ZMB + h)),
            pl.BlockSpec((tk, tk), lambda b, h, i: (0, 0)),
        ],
        out_specs=pl.BlockSpec((tq, V_DIM), lambda b, h, i: (b * n_q + i, h)),
        out_shape=jax.ShapeDtypeStruct((batch * seq, hh * V_DIM), BF16),
        scratch_shapes=[pltpu.VMEM((V_ROWS, tq), F32), pltpu.VMEM((1, tq), F32),
                        pltpu.VMEM((2, tk, tq), F32), pltpu.VMEM((2, tk, tq), BF16),
                        pltpu.VMEM((2, 1, tq), F32), pltpu.VMEM((2, 1, tq), F32)],
        compiler_params=_params("parallel", "parallel", "arbitrary"),
        name="flash",
    )(q_t, k_all, v_t, main, bias)


def _mla_decode_kernel(cq_ref, ckv_ref, sm_ref, z_ref, plat_ref, pkr_ref, cos_ref, sin_ref, qnw_ref, kvnw_ref,
                       wqn_ref, wqr_ref, wqrr_ref, wukt_ref, wuv_ref, o_ref, lat_ref, kr_ref, *, past, t):
    scale = QK_DIM ** -0.5
    cqn = _rms(cq_ref[...].astype(F32), qnw_ref[...]).astype(BF16)
    ckvn32 = _rms(ckv_ref[...].astype(F32), kvnw_ref[...])
    lat_ref[0] = ckvn32
    ckvn = ckvn32.astype(BF16)
    sm = sm_ref[...]
    cos2 = cos_ref[...]
    sin2 = sin_ref[...]
    krope32 = sm[:, 0:ROPE_DIM] * cos2 + sm[:, ROPE_DIM:2 * ROPE_DIM] * sin2
    kr_ref[0] = krope32
    krope = krope32.astype(BF16)

    qlats, qropes = [], []
    for h in range(MLA_HEADS):
        qn = _dot(cqn, wqn_ref[h]).astype(BF16)
        qlats.append((_dot(qn, wukt_ref[h]) * scale).astype(BF16))
        qr = _dot(cqn, wqr_ref[h]) * cos2 + _dot(cqn, wqrr_ref[h]) * sin2
        qropes.append((qr * scale).astype(BF16))
    qlat = jnp.concatenate(qlats, axis=0)
    qrope = jnp.concatenate(qropes, axis=0)

    plat = plat_ref[0].astype(BF16)
    pkr = pkr_ref[0].astype(BF16)
    s_past = _nt(qlat, plat) + _nt(qrope, pkr)
    s_new = _nt(qlat, ckvn) + _nt(qrope, krope)
    rows = MLA_HEADS * t
    qpos = past + jnp.bitwise_and(lax.broadcasted_iota(jnp.int32, (rows, t), 0), t - 1)
    kpos = past + lax.broadcasted_iota(jnp.int32, (rows, t), 1)
    shift = int(math.log2(CHUNK))
    visible = kpos < jnp.left_shift(jnp.right_shift(qpos, shift) + 1, shift)
    s_new = jnp.where(visible, s_new, -jnp.inf)
    m = jnp.maximum(jnp.max(s_past, axis=-1, keepdims=True), jnp.max(s_new, axis=-1, keepdims=True))
    p_past = jnp.exp(s_past - m)
    p_new = jnp.exp(s_new - m)
    l = jnp.sum(p_past, axis=-1, keepdims=True) + jnp.sum(p_new, axis=-1, keepdims=True)
    o_lat = (_dot(p_past.astype(BF16), plat) + _dot(p_new.astype(BF16), ckvn)) / l
    z = z_ref[...].astype(F32)
    gate = z * _sigmoid(z)
    for h in range(MLA_HEADS):
        o_h = _dot(o_lat[h * t:(h + 1) * t].astype(BF16), wuv_ref[h])
        cols = slice(h * V_DIM, (h + 1) * V_DIM)
        o_ref[:, cols] = (o_h * gate[:, cols]).astype(BF16)


def _mla_decode(main, small, past_lat, past_kr, cos2, sin2, qn_w, kvn_w, wq_n, wq_r, wq_rr, wuk_t, wuv, batch, t):
    past = past_lat.shape[1]
    hh = MLA_HEADS
    const = lambda *shape: pl.BlockSpec(shape, lambda b: (0,) * len(shape))
    return pl.pallas_call(
        functools.partial(_mla_decode_kernel, past=past, t=t),
        grid=(batch,),
        in_specs=[
            pl.BlockSpec((t, Q_LORA), lambda b: (b, _CQ_BLK)),
            pl.BlockSpec((t, KV_LORA), lambda b: (b, _CKV_BLK)),
            pl.BlockSpec((t, LANE), lambda b: (b, 0)),
            pl.BlockSpec((t, hh * V_DIM), lambda b: (b, _ZMB // hh)),
            pl.BlockSpec((1, past, KV_LORA), lambda b: (b, 0, 0)),
            pl.BlockSpec((1, past, ROPE_DIM), lambda b: (b, 0, 0)),
            const(t, ROPE_DIM), const(t, ROPE_DIM),
            const(1, Q_LORA), const(1, KV_LORA),
            const(hh, Q_LORA, NOPE_DIM), const(hh, Q_LORA, ROPE_DIM), const(hh, Q_LORA, ROPE_DIM),
            const(hh, NOPE_DIM, KV_LORA), const(hh, KV_LORA, V_DIM),
        ],
        out_specs=[
            pl.BlockSpec((t, hh * V_DIM), lambda b: (b, 0)),
            pl.BlockSpec((1, t, KV_LORA), lambda b: (b, 0, 0)),
            pl.BlockSpec((1, t, ROPE_DIM), lambda b: (b, 0, 0)),
        ],
        out_shape=[
            jax.ShapeDtypeStruct((batch * t, hh * V_DIM), BF16),
            jax.ShapeDtypeStruct((batch, t, KV_LORA), F32),
            jax.ShapeDtypeStruct((batch, t, ROPE_DIM), F32),
        ],
        compiler_params=_params("parallel"),
        name="mla_decode",
    )(main, main, small, main, past_lat, past_kr, cos2, sin2, qn_w, kvn_w, wq_n, wq_r, wq_rr, wuk_t, wuv)


def _rope_tables(p0, t):
    pos = jnp.arange(p0, p0 + t, dtype=jnp.int32)
    inv = ROPE_THETA ** (-jnp.arange(0, ROPE_DIM, 2, dtype=F32) / ROPE_DIM)
    ang = pos.astype(F32)[:, None] * inv[None, :]
    return jnp.cos(ang), jnp.sin(ang)


def _pick(n, prefs):
    for p in prefs:
        if n % p == 0:
            return p
    return n


def _layer(x, conv0, ssm0, past_lat, past_kr, lw, lnf_w):
    (ln_w, w_in, w_conv, a_log, dt_bias, gn_w, qn_w, w_q_up, kvn_w, w_uk, w_uv, w_out) = lw
    batch, t, d = x.shape
    n = batch * t
    hh = MLA_HEADS
    half = ROPE_DIM // 2
    x2d = x.reshape(n, d)

    sizes = (3 * GDN_HEADS * GDN_DK, GDN_HEADS * GDN_DV, GDN_HEADS, GDN_HEADS, Q_LORA, KV_LORA, ROPE_DIM, hh * V_DIM)
    offs = [0]
    for s in sizes:
        offs.append(offs[-1] + s)
    w_qkv, w_zg, w_bg, w_ag, w_cq, w_ckv, w_kr, w_zm = (w_in[:, offs[i]:offs[i + 1]] for i in range(8))
    w_main = jnp.concatenate([w_qkv, w_zg, w_zm, w_cq, w_ckv], axis=1).astype(BF16)
    w_small = jnp.concatenate(
        [w_kr, w_kr[:, half:], w_kr[:, :half], w_bg, w_ag,
         jnp.zeros((d, _SMALL_COLS - 2 * ROPE_DIM - 2 * GDN_HEADS), F32)], axis=1).astype(BF16)
    pad = jnp.zeros((LANE - 2 * GDN_HEADS,), F32)
    gate_rows = jnp.stack([jnp.concatenate([jnp.zeros((GDN_HEADS,), F32), a_log, pad]),
                           jnp.concatenate([jnp.zeros((GDN_HEADS,), F32), dt_bias, pad])])

    tm = _pick(n, (1024, 512, 256))
    main, small = _in_proj(x2d, ln_w.reshape(1, d), w_main, w_small, tm, 2048)

    c = min(GDN_CHUNK, t)
    tb = _pick(t, (256, 128, 64))
    o_g, new_ssm = _gdn(main, small, conv0, ssm0, w_conv, gate_rows, gn_w.reshape(1, GDN_DV), batch, t, tb, c, 8)
    qkv_rows = main.reshape(batch, t, -1)[:, max(t - (CONV_W - 1), 0):, :3 * GDN_HEADS * GDN_DK].astype(F32)
    new_conv = jnp.concatenate([conv0, qkv_rows], axis=1)[:, -(CONV_W - 1):]

    p_len = past_lat.shape[1]
    cos, sin = _rope_tables(p_len, t)
    cos2 = jnp.concatenate([cos, cos], axis=1)
    sin2 = jnp.concatenate([-sin, sin], axis=1)
    qn_w2 = qn_w.reshape(1, Q_LORA)
    kvn_w2 = kvn_w.reshape(1, KV_LORA)
    if p_len == 0:
        wq_t = jnp.transpose(w_q_up, (1, 2, 0)).reshape(hh * QK_DIM, Q_LORA).astype(BF16)
        w_uk2 = w_uk.reshape(KV_LORA, hh * NOPE_DIM).astype(BF16)
        wuv_t = jnp.transpose(w_uv, (1, 2, 0)).reshape(hh * V_DIM, KV_LORA).astype(BF16)
        tmq = _pick(t, (512, 256, 128))
        q_t, k_all, v_t, lat, kr = _mla_prep(main, small, cos.T, sin.T, cos2, sin2, qn_w2, kvn_w2,
                                             wq_t, w_uk2, wuv_t, batch, t, tmq)
        tq = _pick(t, (2048, 1024, 512))
        o_m = _flash(q_t, k_all, v_t, main, batch, t, tq, 256)
    else:
        wq_n = jnp.transpose(w_q_up[:, :, :NOPE_DIM], (1, 0, 2)).astype(BF16)
        wq_r = jnp.transpose(w_q_up[:, :, NOPE_DIM:], (1, 0, 2))
        wq_rr = jnp.concatenate([wq_r[..., half:], wq_r[..., :half]], axis=-1).astype(BF16)
        wuk_t = jnp.transpose(w_uk, (1, 2, 0)).astype(BF16)
        wuv = jnp.transpose(w_uv, (1, 0, 2)).astype(BF16)
        o_m, lat, kr = _mla_decode(main, small, past_lat, past_kr, cos2, sin2, qn_w2, kvn_w2,
                                   wq_n, wq_r.astype(BF16), wq_rr, wuk_t, wuv, batch, t)

    w_g = w_out[:GDN_HEADS * GDN_DV].astype(BF16)
    w_m = w_out[GDN_HEADS * GDN_DV:].astype(BF16)
    y = _out_proj(x2d, o_g, o_m, w_g, w_m, lnf_w.reshape(1, d), _pick(n, (512, 256)))
    return y.reshape(batch, t, d), new_conv, new_ssm, lat, kr


def kernel(x_prompt, x_sample, cache_mla_latent, cache_mla_krope, state_gdn_conv, state_gdn_ssm, ln_in_w, w_in, w_conv, a_log, dt_bias, gdn_norm_w, q_norm_w, w_q_up, kv_norm_w, w_uk, w_uv, w_out, ln_final_w):
    depth = w_in.shape[0]
    assert depth == 1, "the final norm is fused into the out-projection of the single layer"
    bp = x_prompt.shape[0]
    lw = (ln_in_w[0], w_in[0], w_conv[0], a_log[0], dt_bias[0], gdn_norm_w[0],
          q_norm_w[0], w_q_up[0], kv_norm_w[0], w_uk[0], w_uv[0], w_out[0])
    zc = jnp.zeros((bp, CONV_W - 1, 3 * GDN_HEADS * GDN_DK), F32)
    zs = jnp.zeros((bp, GDN_HEADS, GDN_DK, GDN_DV), F32)
    yp, c1, s1, l1, r1 = _layer(x_prompt, zc, zs, jnp.zeros((bp, 0, KV_LORA), F32),
                                jnp.zeros((bp, 0, ROPE_DIM), F32), lw, ln_final_w)
    ys, c2, s2, l2, r2 = _layer(x_sample, state_gdn_conv[0], state_gdn_ssm[0], cache_mla_latent[0],
                                cache_mla_krope[0], lw, ln_final_w)
    return (yp, ys, c1[None], s1[None], l1[None], r1[None], c2[None], s2[None], l2[None], r2[None])
```

```python
import functools
import math

import jax
import jax.numpy as jnp
from jax import lax
from jax.experimental import pallas as pl
from jax.experimental.pallas import tpu as pltpu

F32 = jnp.float32
BF16 = jnp.bfloat16

EPS = 1e-6
CHUNK = 64
GDN_CHUNK = 64
GDN_HEADS = 8
GDN_DK = 128
GDN_DV = 128
CONV_W = 4
MLA_HEADS = 8
Q_LORA = 512
KV_LORA = 512
NOPE_DIM = 128
ROPE_DIM = 64
V_DIM = 128
ROPE_THETA = 10000.0
QK_DIM = NOPE_DIM + ROPE_DIM
V_ROWS = V_DIM + 16

LANE = 128
MXU_COLS = 256
VMEM_LIMIT = 56 * 1024 * 1024

_QB, _KB, _VB, _ZGB, _ZMB = 0, 8, 16, 24, 32
_CQ_BLK, _CKV_BLK = 10, 11
_MAIN_COLS = 6144
_SMALL_COLS = 256


def _nt(a, b):
    return lax.dot_general(a, b, (((1,), (1,)), ((), ())), preferred_element_type=F32)


def _tn(a, b):
    return lax.dot_general(a, b, (((0,), (0,)), ((), ())), preferred_element_type=F32)


def _dot(a, b):
    return jnp.dot(a, b, preferred_element_type=F32)


def _rms(x, w):
    return x * lax.rsqrt(jnp.mean(x * x, axis=-1, keepdims=True) + EPS) * w


def _sigmoid(x):
    return 1.0 / (1.0 + jnp.exp(-x))


def _params(*sem):
    return pltpu.CompilerParams(dimension_semantics=sem, vmem_limit_bytes=VMEM_LIMIT)


def _in_proj_kernel(x_ref, lnw_ref, wm_ref, ws_ref, main_ref, small_ref, h_ref, *, rows):
    @pl.when(pl.program_id(1) == 0)
    def _():
        def body(r, c):
            sl = pl.ds(pl.multiple_of(r * rows, rows), rows)
            h_ref[sl, :] = _rms(x_ref[sl, :], lnw_ref[...]).astype(BF16)
            return c
        lax.fori_loop(0, x_ref.shape[0] // rows, body, 0)
        small_ref[...] = _dot(h_ref[...], ws_ref[...])

    main_ref[...] = _dot(h_ref[...], wm_ref[...]).astype(BF16)


def _in_proj(x2d, ln_w, w_main, w_small, tm, tn):
    n, d = x2d.shape
    nm, ns = w_main.shape[1], w_small.shape[1]
    rows = min(tm, 128)
    return pl.pallas_call(
        functools.partial(_in_proj_kernel, rows=rows),
        grid=(n // tm, nm // tn),
        in_specs=[
            pl.BlockSpec((tm, d), lambda i, j: (i, 0)),
            pl.BlockSpec((1, d), lambda i, j: (0, 0)),
            pl.BlockSpec((d, tn), lambda i, j: (0, j)),
            pl.BlockSpec((d, ns), lambda i, j: (0, 0)),
        ],
        out_specs=[
            pl.BlockSpec((tm, tn), lambda i, j: (i, j)),
            pl.BlockSpec((tm, ns), lambda i, j: (i, 0)),
        ],
        out_shape=[
            jax.ShapeDtypeStruct((n, nm), BF16),
            jax.ShapeDtypeStruct((n, ns), F32),
        ],
        scratch_shapes=[pltpu.VMEM((tm, d), BF16)],
        compiler_params=_params("parallel", "arbitrary"),
        name="in_proj",
    )(x2d, ln_w, w_main, w_small)


def _out_proj_kernel(x_ref, og_ref, om_ref, wg_ref, wm_ref, lnf_ref, y_ref):
    acc = _dot(og_ref[...], wg_ref[...]) + _dot(om_ref[...], wm_ref[...])
    y_ref[...] = _rms(x_ref[...] + acc, lnf_ref[...])


def _out_proj(x2d, o_g, o_m, w_g, w_m, lnf_w, tm):
    n, d = x2d.shape
    kg, km = o_g.shape[1], o_m.shape[1]
    return pl.pallas_call(
        _out_proj_kernel,
        grid=(n // tm,),
        in_specs=[
            pl.BlockSpec((tm, d), lambda i: (i, 0)),
            pl.BlockSpec((tm, kg), lambda i: (i, 0)),
            pl.BlockSpec((tm, km), lambda i: (i, 0)),
            pl.BlockSpec((kg, d), lambda i: (0, 0)),
            pl.BlockSpec((km, d), lambda i: (0, 0)),
            pl.BlockSpec((1, d), lambda i: (0, 0)),
        ],
        out_specs=pl.BlockSpec((tm, d), lambda i: (i, 0)),
        out_shape=jax.ShapeDtypeStruct((n, d), F32),
        compiler_params=_params("parallel"),
        name="out_proj",
    )(x2d, o_g, o_m, w_g, w_m, lnf_w)


def _gdn_kernel(q_ref, k_ref, v_ref, z_ref, sm_ref, cq0_ref, ck0_ref, cv0_ref, s0_ref,
                wq_ref, wk_ref, wv_ref, gp_ref, gn_ref, o_ref, sout_ref, xbuf, s_ref, *, tb, c, hb):
    hg = pl.program_id(1)
    t = pl.program_id(2)
    dk = GDN_DK
    wd = hb * dk

    @pl.when(t == 0)
    def _():
        xbuf[0:8, :] = jnp.zeros((8, 3 * wd), F32)
        xbuf[8 - (CONV_W - 1):8, 0:wd] = cq0_ref[0]
        xbuf[8 - (CONV_W - 1):8, wd:2 * wd] = ck0_ref[0]
        xbuf[8 - (CONV_W - 1):8, 2 * wd:3 * wd] = cv0_ref[0]
        s_ref[...] = s0_ref[0]

    xbuf[8:8 + tb, 0:wd] = q_ref[...].astype(F32)
    xbuf[8:8 + tb, wd:2 * wd] = k_ref[...].astype(F32)
    xbuf[8:8 + tb, 2 * wd:3 * wd] = v_ref[...].astype(F32)

    srow = lax.broadcasted_iota(jnp.int32, (tb, tb), 0)
    scol = lax.broadcasted_iota(jnp.int32, (tb, tb), 1)
    shifts = [jnp.where(scol == srow - i, 1.0, 0.0).astype(BF16) for i in range(1, CONV_W)]
    gw = 2 if hb % 2 == 0 else 1
    row8 = lax.broadcasted_iota(jnp.int32, (8, gw * dk), 0)

    def conv(idx, x_ref, w_ref, hp):
        lo = hp * gw * dk
        w = w_ref[:, lo:lo + gw * dk]
        cols = slice(idx * wd + lo, idx * wd + lo + gw * dk)
        y = xbuf[8:8 + tb, cols] * w[CONV_W - 1:CONV_W]
        head = jnp.zeros((8, gw * dk), F32)
        for i in range(1, CONV_W):
            wi = w[CONV_W - 1 - i:CONV_W - i]
            y = y + _dot(shifts[i - 1], x_ref[:, lo:lo + gw * dk]) * wi
            head = head + jnp.where(row8 < i, xbuf[8 - i:16 - i, cols], 0.0) * wi
        y = jnp.concatenate([y[0:8] + head, y[8:]], axis=0)
        return y * _sigmoid(y)

    sm = sm_ref[...]
    gp = gp_ref[...]
    beta_all = _sigmoid(sm)
    xg = sm + gp[1:2]
    softplus = jnp.maximum(xg, 0.0) + jnp.log(1.0 + jnp.exp(-jnp.abs(xg)))
    g_all = -jnp.exp(gp[0:1]) * softplus
    lane = lax.broadcasted_iota(jnp.int32, (tb, LANE), 1)

    shift = int(math.log2(c))
    nc = tb // c
    row = lax.broadcasted_iota(jnp.int32, (tb, tb), 0)
    col = lax.broadcasted_iota(jnp.int32, (tb, tb), 1)
    same = jnp.right_shift(row, shift) == jnp.right_shift(col, shift)
    lincl = jnp.where(jnp.logical_and(same, col <= row), 1.0, 0.0).astype(BF16)
    bd_mask = jnp.where(same, 1.0, 0.0).astype(BF16)
    rl = lax.broadcasted_iota(jnp.int32, (c, tb), 0)
    ll = lax.broadcasted_iota(jnp.int32, (c, tb), 1)
    jl = jnp.bitwise_and(ll, c - 1)
    lb = jnp.right_shift(ll, shift)
    incl = jl <= rl
    strict = jl < rl
    eye = jnp.where(jl == rl, 1.0, 0.0)

    def to_ls(x):
        out = x[(nc - 1) * c:nc * c]
        for n in range(nc - 2, -1, -1):
            out = jnp.where(lb == n, x[n * c:(n + 1) * c], out)
        return out

    def to_bd(x_b):
        if nc == 1:
            return x_b
        return jnp.concatenate([x_b] * nc, axis=0) * bd_mask

    def widen(x):
        return x[:, :tb] if tb <= dk else jnp.concatenate([x] * (tb // dk), axis=1)

    hs = range(hb)
    beta = [jnp.sum(jnp.where(lane == hg * hb + h, beta_all, 0.0), axis=-1, keepdims=True) for h in hs]
    g = [jnp.sum(jnp.where(lane == hg * hb + h + GDN_HEADS, g_all, 0.0), axis=-1, keepdims=True) for h in hs]

    g_hi = [x.astype(BF16).astype(F32) for x in g]
    g_lo = [(x - y).astype(BF16).astype(F32) for x, y in zip(g, g_hi)]
    gsum = [_dot(lincl, jnp.concatenate([jnp.broadcast_to(x, (tb, dk)), jnp.broadcast_to(y, (tb, dk))],
                                        axis=1).astype(BF16)) for x, y in zip(g_hi, g_lo)]
    gcb = [x[:, :dk] + x[:, dk:] for x in gsum]
    gc_row = [x.T[0:1, :] for x in gcb]
    decay = [jnp.where(incl, jnp.exp(jnp.where(incl, to_ls(widen(x)) - y, 0.0)), 0.0)
             for x, y in zip(gcb, gc_row)]
    egc = [jnp.exp(x) for x in gcb]
    glast = [[x[(ci + 1) * c - 1:(ci + 1) * c, :] for ci in range(nc)] for x in gcb]
    cat = lambda xs: xs[0] if len(xs) == 1 else jnp.concatenate(xs, axis=0)
    kfac = [cat([jnp.exp(glast[h][ci] - gcb[h][ci * c:(ci + 1) * c]) for ci in range(nc)]) for h in hs]

    k_b, kb_b, kbg_b, q_b, qg_b, vb_b, kdec_b = ([None] * hb for _ in range(7))
    for hp in range(hb // gw):
        q2 = conv(0, q_ref, wq_ref, hp)
        k2 = conv(1, k_ref, wk_ref, hp)
        v2 = conv(2, v_ref, wv_ref, hp)
        for j in range(gw):
            h = hp * gw + j
            sl = slice(j * dk, (j + 1) * dk)
            qh, kh, vh = q2[:, sl], k2[:, sl], v2[:, sl]
            qh = qh * (lax.rsqrt(jnp.sum(qh * qh, axis=-1, keepdims=True) + EPS) * (dk ** -0.5))
            kh = kh * lax.rsqrt(jnp.sum(kh * kh, axis=-1, keepdims=True) + EPS)
            kbh = kh * beta[h]
            k_b[h] = kh.astype(BF16)
            kb_b[h] = kbh.astype(BF16)
            kbg_b[h] = (kbh * egc[h]).astype(BF16)
            q_b[h] = qh.astype(BF16)
            qg_b[h] = (qh * egc[h]).astype(BF16)
            vb_b[h] = (vh * beta[h]).astype(BF16)
            kdec_b[h] = (kh * kfac[h]).astype(BF16)

    kk = [to_ls(_nt(x, y)) for x, y in zip(kb_b, k_b)]
    p = [jnp.where(strict, -(x * y), 0.0) for x, y in zip(kk, decay)]
    tmat = [eye + x for x in p]
    p = [_dot(x.astype(BF16), to_bd(x.astype(BF16))) for x in p]
    for lvl in range(1, shift):
        p_b = [x.astype(BF16) for x in p]
        p_bd = [to_bd(x) for x in p_b]
        if lvl < shift - 1:
            both = [_dot(jnp.concatenate([x, y.astype(BF16)], axis=0), z) for x, y, z in zip(p_b, tmat, p_bd)]
            p = [x[:c] for x in both]
            tmat = [x + y[c:] for x, y in zip(tmat, both)]
        else:
            tmat = [x + _dot(x.astype(BF16), y) for x, y in zip(tmat, p_bd)]
    t_bd = [to_bd(x.astype(BF16)) for x in tmat]
    uw = [_dot(x, jnp.concatenate([y, z], axis=1)) for x, y, z in zip(t_bd, vb_b, kbg_b)]
    u = [x[:, :GDN_DV] for x in uw]
    w_b = [x[:, GDN_DV:].astype(BF16) for x in uw]
    attn = [to_bd(jnp.where(incl, to_ls(_nt(x, y)) * z, 0.0).astype(BF16))
            for x, y, z in zip(q_b, k_b, decay)]

    s = [s_ref[h] for h in hs]
    vnews = [[] for _ in hs]
    inters = [[] for _ in hs]
    for ci in range(tb // c):
        rows = slice(ci * c, (ci + 1) * c)
        s_b = [x.astype(BF16) for x in s]
        both = [_dot(jnp.concatenate([w_b[h][rows], qg_b[h][rows]], axis=0), s_b[h]) for h in hs]
        v_new = [u[h][rows] - both[h][:c] for h in hs]
        for h in hs:
            inters[h].append(both[h][c:])
            vnews[h].append(v_new[h])
        s = [s[h] * jnp.exp(glast[h][ci]) + _tn(kdec_b[h][rows], v_new[h].astype(BF16)) for h in hs]
    for h in hs:
        s_ref[h] = s[h]
        sout_ref[0, h] = s[h]
    o = [cat(inters[h]) + _dot(attn[h], cat(vnews[h]).astype(BF16)) for h in hs]
    for h in hs:
        z = z_ref[:, h * dk:(h + 1) * dk].astype(F32)
        o_ref[:, h * dk:(h + 1) * dk] = (_rms(o[h], gn_ref[...]) * (z * _sigmoid(z))).astype(BF16)

    xbuf[0:8, :] = xbuf[tb:tb + 8, :]


def _gdn(main, small, conv0, ssm0, w_conv, gate_rows, gnorm_w, batch, seq, tb, c, hb):
    n_t = seq // tb
    hh = GDN_HEADS
    dk = GDN_DK
    wd = hb * dk

    def blk(off):
        return pl.BlockSpec((tb, wd), lambda b, h, t: (b * n_t + t, off // hb + h))

    def cblk(off):
        return pl.BlockSpec((1, CONV_W - 1, wd), lambda b, h, t: (b, 0, off // hb + h))

    def wblk(off):
        return pl.BlockSpec((CONV_W, wd), lambda b, h, t: (0, off // hb + h))

    return pl.pallas_call(
        functools.partial(_gdn_kernel, tb=tb, c=c, hb=hb),
        grid=(batch, hh // hb, n_t),
        in_specs=[
            blk(_QB), blk(_KB), blk(_VB), blk(_ZGB),
            pl.BlockSpec((tb, LANE), lambda b, h, t: (b * n_t + t, 1)),
            cblk(_QB), cblk(_KB), cblk(_VB),
            pl.BlockSpec((1, hb, dk, GDN_DV), lambda b, h, t: (b, h, 0, 0)),
            wblk(_QB), wblk(_KB), wblk(_VB),
            pl.BlockSpec((2, LANE), lambda b, h, t: (0, 0)),
            pl.BlockSpec((1, GDN_DV), lambda b, h, t: (0, 0)),
        ],
        out_specs=[
            pl.BlockSpec((tb, wd), lambda b, h, t: (b * n_t + t, h)),
            pl.BlockSpec((1, hb, dk, GDN_DV), lambda b, h, t: (b, h, 0, 0)),
        ],
        out_shape=[
            jax.ShapeDtypeStruct((batch * seq, hh * GDN_DV), BF16),
            jax.ShapeDtypeStruct((batch, hh, dk, GDN_DV), F32),
        ],
        scratch_shapes=[pltpu.VMEM((tb + 8, 3 * wd), F32), pltpu.VMEM((hb, dk, GDN_DV), F32)],
        compiler_params=_params("parallel", "parallel", "arbitrary"),
        name="gdn",
    )(main, main, main, main, small, conv0, conv0, conv0, ssm0, w_conv, w_conv, w_conv, gate_rows, gnorm_w)


def _mla_prep_kernel(cq_ref, ckv_ref, sm_ref, cosq_ref, sinq_ref, cosk_ref, sink_ref, qnw_ref, kvnw_ref,
                     wqt_ref, wuk_ref, wuvt_ref, qt_ref, k_ref, vt_ref, lat_ref, kr_ref):
    scale = QK_DIM ** -0.5 * math.log2(math.e)
    half = ROPE_DIM // 2
    tm = cq_ref.shape[0]
    cqn = _rms(cq_ref[...].astype(F32), qnw_ref[...]).astype(BF16)
    ckvn32 = _rms(ckv_ref[...].astype(F32), kvnw_ref[...])
    lat_ref[0] = ckvn32
    ckvn = ckvn32.astype(BF16)
    sm = sm_ref[...]
    krope32 = sm[:, 0:ROPE_DIM] * cosk_ref[...] + sm[:, ROPE_DIM:2 * ROPE_DIM] * sink_ref[...]
    kr_ref[0] = krope32
    krope = krope32.astype(BF16)
    knope = _dot(ckvn, wuk_ref[...])
    cq_cos = cosq_ref[...]
    cq_sin = sinq_ref[...]
    vt_all = _nt(wuvt_ref[...], ckvn)
    qt_all = _nt(wqt_ref[...], cqn)
    for h in range(MLA_HEADS):
        k_ref[0, h, :, 0:NOPE_DIM] = knope[:, h * NOPE_DIM:(h + 1) * NOPE_DIM].astype(BF16)
        k_ref[0, h, :, NOPE_DIM:QK_DIM] = krope
        vt_ref[0, h, 0:V_DIM, :] = vt_all[h * V_DIM:(h + 1) * V_DIM].astype(BF16)
        vt_ref[0, h, V_DIM:V_ROWS, :] = jnp.ones((V_ROWS - V_DIM, tm), BF16)
        qt = qt_all[h * QK_DIM:(h + 1) * QK_DIM]
        x1 = qt[NOPE_DIM:NOPE_DIM + half]
        x2 = qt[NOPE_DIM + half:QK_DIM]
        qt_ref[0, h, 0:NOPE_DIM, :] = (qt[0:NOPE_DIM] * scale).astype(BF16)
        qt_ref[0, h, NOPE_DIM:NOPE_DIM + half, :] = ((x1 * cq_cos - x2 * cq_sin) * scale).astype(BF16)
        qt_ref[0, h, NOPE_DIM + half:QK_DIM, :] = ((x2 * cq_cos + x1 * cq_sin) * scale).astype(BF16)


def _mla_prep(main, small, cos_t, sin_t, cos2, sin2, qn_w, kvn_w, wq_t, w_uk2, wuv_t, batch, seq, tm):
    n_t = seq // tm
    hh = MLA_HEADS
    half = ROPE_DIM // 2
    const = lambda *shape: pl.BlockSpec(shape, lambda b, i: (0,) * len(shape))
    return pl.pallas_call(
        _mla_prep_kernel,
        grid=(batch, n_t),
        in_specs=[
            pl.BlockSpec((tm, Q_LORA), lambda b, i: (b * n_t + i, _CQ_BLK)),
            pl.BlockSpec((tm, KV_LORA), lambda b, i: (b * n_t + i, _CKV_BLK)),
            pl.BlockSpec((tm, LANE), lambda b, i: (b * n_t + i, 0)),
            pl.BlockSpec((half, tm), lambda b, i: (0, i)),
            pl.BlockSpec((half, tm), lambda b, i: (0, i)),
            pl.BlockSpec((tm, ROPE_DIM), lambda b, i: (i, 0)),
            pl.BlockSpec((tm, ROPE_DIM), lambda b, i: (i, 0)),
            const(1, Q_LORA), const(1, KV_LORA),
            const(hh * QK_DIM, Q_LORA), const(KV_LORA, hh * NOPE_DIM), const(hh * V_DIM, KV_LORA),
        ],
        out_specs=[
            pl.BlockSpec((1, hh, QK_DIM, tm), lambda b, i: (b, 0, 0, i)),
            pl.BlockSpec((1, hh, tm, QK_DIM), lambda b, i: (b, 0, i, 0)),
            pl.BlockSpec((1, hh, V_ROWS, tm), lambda b, i: (b, 0, 0, i)),
            pl.BlockSpec((1, tm, KV_LORA), lambda b, i: (b, i, 0)),
            pl.BlockSpec((1, tm, ROPE_DIM), lambda b, i: (b, i, 0)),
        ],
        out_shape=[
            jax.ShapeDtypeStruct((batch, hh, QK_DIM, seq), BF16),
            jax.ShapeDtypeStruct((batch, hh, seq, QK_DIM), BF16),
            jax.ShapeDtypeStruct((batch, hh, V_ROWS, seq), BF16),
            jax.ShapeDtypeStruct((batch, seq, KV_LORA), F32),
            jax.ShapeDtypeStruct((batch, seq, ROPE_DIM), F32),
        ],
        compiler_params=_params("parallel", "parallel"),
        name="mla_prep",
    )(main, main, small, cos_t, sin_t, cos2, sin2, qn_w, kvn_w, wq_t, w_uk2, wuv_t)


_NEG = -1e30


def _flash_kernel(qt_ref, k_ref, vt_ref, z_ref, bias_ref, o_ref, acc_ref, m_ref, s_buf, p_buf, mb_buf, al_buf,
                  *, tq, tk, unroll):
    r = tq // tk
    strip = min(tk, MXU_COLS)
    i = pl.program_id(2)
    nb = r * (i + 1)

    m_ref[...] = jnp.full(m_ref.shape, _NEG, F32)
    acc_ref[...] = jnp.zeros(acc_ref.shape, F32)
    p_buf[...] = jnp.zeros(p_buf.shape, BF16)
    al_buf[...] = jnp.ones(al_buf.shape, F32)
    s_buf[1] = jnp.full(s_buf.shape[1:], _NEG, F32)
    mb_buf[1] = jnp.full(mb_buf.shape[1:], _NEG, F32)

    def stage_a(blk, slot, u, cols):
        off = pl.multiple_of(blk * tk, tk)
        s = _dot(k_ref[0, 0, pl.ds(off, tk), :], qt_ref[0, 0, :, cols])
        if u >= 0 and cols.start < (u + 1) * tk:
            s = s + bias_ref[:, cols.start - u * tk:cols.stop - u * tk]
        s_buf[slot, :, cols] = s
        mb_buf[slot, :, cols] = jnp.max(s, axis=0, keepdims=True)

    def stage_b(slot, cols):
        m_prev = m_ref[:, cols]
        m_new = jnp.maximum(m_prev, mb_buf[slot, :, cols])
        al_buf[slot, :, cols] = jnp.exp2(m_prev - m_new)
        p_buf[slot, :, cols] = jnp.exp2(s_buf[slot, :, cols] - m_new).astype(BF16)
        m_ref[:, cols] = m_new

    def stage_c(blk, slot, cols):
        off = pl.multiple_of(jnp.maximum(blk, 0) * tk, tk)
        acc_ref[:, cols] = (acc_ref[:, cols] * al_buf[slot, :, cols]
                            + _dot(vt_ref[0, 0, :, pl.ds(off, tk)], p_buf[slot, :, cols]))

    def half_step(c, a, b):
        for c0 in range(0, tq, strip):
            cols = slice(c0, c0 + strip)
            if c is not None and c0 >= max(c[2], 0) * tk:
                stage_c(c[0], c[1], cols)
            if a is not None and c0 >= max(a[2], 0) * tk:
                stage_a(a[0], a[1], a[2], cols)
            if b is not None and c0 >= max(b[1], 0) * tk:
                stage_b(b[0], cols)

    def pair(e, u):
        half_step((e - 2, 0, u - 2), (e, 0, u), (1, u - 1))
        half_step((e - 1, 1, u - 1), (e + 1, 1, u + 1 if u >= 0 else u), (0, u))

    n_pairs = i * (r // 2)
    n_rem = n_pairs % unroll

    def body_rem(jp, carry):
        pair(2 * jp, -4)
        return carry

    def body(jp, carry):
        for x in range(unroll):
            pair(2 * (n_rem + unroll * jp + x), -4)
        return carry

    lax.fori_loop(0, n_rem, body_rem, 0)
    lax.fori_loop(0, n_pairs // unroll, body, 0)
    for x in range(r // 2):
        pair(nb - r + 2 * x, 2 * x)
    half_step((nb - 2, 0, r - 2), None, (1, r - 1))
    half_step((nb - 1, 1, r - 1), None, None)

    o = (acc_ref[0:V_DIM, :] / acc_ref[V_DIM:V_DIM + 1, :]).T
    z = z_ref[...].astype(F32)
    o_ref[...] = (o * (z * _sigmoid(z))).astype(BF16)


def _flash(q_t, k_all, v_t, main, batch, seq, tq, tk):
    n_q = seq // tq
    r = tq // tk
    unroll = 2
    assert r % 2 == 0 and tq % tk == 0
    hh = MLA_HEADS
    shift = int(math.log2(CHUNK))
    cid = jnp.right_shift(jnp.arange(tk, dtype=jnp.int32), shift)
    bias = jnp.where(cid[:, None] <= cid[None, :], 0.0, _NEG).astype(F32)
    return pl.pallas_call(
        functools.partial(_flash_kernel, tq=tq, tk=tk, unroll=unroll),
        grid=(batch, hh, n_q),
        in_specs=[
            pl.BlockSpec((1, 1, QK_DIM, tq), lambda b, h, i: (b, h, 0, i)),
            pl.BlockSpec((1, 1, seq, QK_DIM), lambda b, h, i: (b, h, 0, 0)),
            pl.BlockSpec((1, 1, V_ROWS, seq), lambda b, h, i: (b, h, 0, 0)),
            pl.BlockSpec((tq, V_DIM), lambda b, h, i: (b * n_q + i, _ZMB + h)),
            pl.BlockSpec((tk, tk), lambda b, h, i: (0, 0)),
        ],
        out_specs=pl.BlockSpec((tq, V_DIM), lambda b, h, i: (b * n_q + i, h)),
        out_shape=jax.ShapeDtypeStruct((batch * seq, hh * V_DIM), BF16),
        scratch_shapes=[pltpu.VMEM((V_ROWS, tq), F32), pltpu.VMEM((1, tq), F32),
                        pltpu.VMEM((2, tk, tq), F32), pltpu.VMEM((2, tk, tq), BF16),
                        pltpu.VMEM((2, 1, tq), F32), pltpu.VMEM((2, 1, tq), F32)],
        compiler_params=_params("parallel", "parallel", "arbitrary"),
        name="flash",
    )(q_t, k_all, v_t, main, bias)


def _mla_decode_kernel(cq_ref, ckv_ref, sm_ref, z_ref, plat_ref, pkr_ref, cos_ref, sin_ref, qnw_ref, kvnw_ref,
                       wqn_ref, wqr_ref, wqrr_ref, wukt_ref, wuv_ref, o_ref, lat_ref, kr_ref, *, past, t):
    scale = QK_DIM ** -0.5
    cqn = _rms(cq_ref[...].astype(F32), qnw_ref[...]).astype(BF16)
    ckvn32 = _rms(ckv_ref[...].astype(F32), kvnw_ref[...])
    lat_ref[0] = ckvn32
    ckvn = ckvn32.astype(BF16)
    sm = sm_ref[...]
    cos2 = cos_ref[...]
    sin2 = sin_ref[...]
    krope32 = sm[:, 0:ROPE_DIM] * cos2 + sm[:, ROPE_DIM:2 * ROPE_DIM] * sin2
    kr_ref[0] = krope32
    krope = krope32.astype(BF16)

    qlats, qropes = [], []
    for h in range(MLA_HEADS):
        qn = _dot(cqn, wqn_ref[h]).astype(BF16)
        qlats.append((_dot(qn, wukt_ref[h]) * scale).astype(BF16))
        qr = _dot(cqn, wqr_ref[h]) * cos2 + _dot(cqn, wqrr_ref[h]) * sin2
        qropes.append((qr * scale).astype(BF16))
    qlat = jnp.concatenate(qlats, axis=0)
    qrope = jnp.concatenate(qropes, axis=0)

    plat = plat_ref[0].astype(BF16)
    pkr = pkr_ref[0].astype(BF16)
    s_past = _nt(qlat, plat) + _nt(qrope, pkr)
    s_new = _nt(qlat, ckvn) + _nt(qrope, krope)
    rows = MLA_HEADS * t
    qpos = past + jnp.bitwise_and(lax.broadcasted_iota(jnp.int32, (rows, t), 0), t - 1)
    kpos = past + lax.broadcasted_iota(jnp.int32, (rows, t), 1)
    shift = int(math.log2(CHUNK))
    visible = kpos < jnp.left_shift(jnp.right_shift(qpos, shift) + 1, shift)
    s_new = jnp.where(visible, s_new, -jnp.inf)
    m = jnp.maximum(jnp.max(s_past, axis=-1, keepdims=True), jnp.max(s_new, axis=-1, keepdims=True))
    p_past = jnp.exp(s_past - m)
    p_new = jnp.exp(s_new - m)
    l = jnp.sum(p_past, axis=-1, keepdims=True) + jnp.sum(p_new, axis=-1, keepdims=True)
    o_lat = (_dot(p_past.astype(BF16), plat) + _dot(p_new.astype(BF16), ckvn)) / l
    z = z_ref[...].astype(F32)
    gate = z * _sigmoid(z)
    for h in range(MLA_HEADS):
        o_h = _dot(o_lat[h * t:(h + 1) * t].astype(BF16), wuv_ref[h])
        cols = slice(h * V_DIM, (h + 1) * V_DIM)
        o_ref[:, cols] = (o_h * gate[:, cols]).astype(BF16)


def _mla_decode(main, small, past_lat, past_kr, cos2, sin2, qn_w, kvn_w, wq_n, wq_r, wq_rr, wuk_t, wuv, batch, t):
    past = past_lat.shape[1]
    hh = MLA_HEADS
    const = lambda *shape: pl.BlockSpec(shape, lambda b: (0,) * len(shape))
    return pl.pallas_call(
        functools.partial(_mla_decode_kernel, past=past, t=t),
        grid=(batch,),
        in_specs=[
            pl.BlockSpec((t, Q_LORA), lambda b: (b, _CQ_BLK)),
            pl.BlockSpec((t, KV_LORA), lambda b: (b, _CKV_BLK)),
            pl.BlockSpec((t, LANE), lambda b: (b, 0)),
            pl.BlockSpec((t, hh * V_DIM), lambda b: (b, _ZMB // hh)),
            pl.BlockSpec((1, past, KV_LORA), lambda b: (b, 0, 0)),
            pl.BlockSpec((1, past, ROPE_DIM), lambda b: (b, 0, 0)),
            const(t, ROPE_DIM), const(t, ROPE_DIM),
            const(1, Q_LORA), const(1, KV_LORA),
            const(hh, Q_LORA, NOPE_DIM), const(hh, Q_LORA, ROPE_DIM), const(hh, Q_LORA, ROPE_DIM),
            const(hh, NOPE_DIM, KV_LORA), const(hh, KV_LORA, V_DIM),
        ],
        out_specs=[
            pl.BlockSpec((t, hh * V_DIM), lambda b: (b, 0)),
            pl.BlockSpec((1, t, KV_LORA), lambda b: (b, 0, 0)),
            pl.BlockSpec((1, t, ROPE_DIM), lambda b: (b, 0, 0)),
        ],
        out_shape=[
            jax.ShapeDtypeStruct((batch * t, hh * V_DIM), BF16),
            jax.ShapeDtypeStruct((batch, t, KV_LORA), F32),
            jax.ShapeDtypeStruct((batch, t, ROPE_DIM), F32),
        ],
        compiler_params=_params("parallel"),
        name="mla_decode",
    )(main, main, small, main, past_lat, past_kr, cos2, sin2, qn_w, kvn_w, wq_n, wq_r, wq_rr, wuk_t, wuv)


def _rope_tables(p0, t):
    pos = jnp.arange(p0, p0 + t, dtype=jnp.int32)
    inv = ROPE_THETA ** (-jnp.arange(0, ROPE_DIM, 2, dtype=F32) / ROPE_DIM)
    ang = pos.astype(F32)[:, None] * inv[None, :]
    return jnp.cos(ang), jnp.sin(ang)


def _pick(n, prefs):
    for p in prefs:
        if n % p == 0:
            return p
    return n


def _layer(x, conv0, ssm0, past_lat, past_kr, lw, lnf_w):
    (ln_w, w_in, w_conv, a_log, dt_bias, gn_w, qn_w, w_q_up, kvn_w, w_uk, w_uv, w_out) = lw
    batch, t, d = x.shape
    n = batch * t
    hh = MLA_HEADS
    half = ROPE_DIM // 2
    x2d = x.reshape(n, d)

    sizes = (3 * GDN_HEADS * GDN_DK, GDN_HEADS * GDN_DV, GDN_HEADS, GDN_HEADS, Q_LORA, KV_LORA, ROPE_DIM, hh * V_DIM)
    offs = [0]
    for s in sizes:
        offs.append(offs[-1] + s)
    w_qkv, w_zg, w_bg, w_ag, w_cq, w_ckv, w_kr, w_zm = (w_in[:, offs[i]:offs[i + 1]] for i in range(8))
    w_main = jnp.concatenate([w_qkv, w_zg, w_zm, w_cq, w_ckv], axis=1).astype(BF16)
    w_small = jnp.concatenate(
        [w_kr, w_kr[:, half:], w_kr[:, :half], w_bg, w_ag,
         jnp.zeros((d, _SMALL_COLS - 2 * ROPE_DIM - 2 * GDN_HEADS), F32)], axis=1).astype(BF16)
    pad = jnp.zeros((LANE - 2 * GDN_HEADS,), F32)
    gate_rows = jnp.stack([jnp.concatenate([jnp.zeros((GDN_HEADS,), F32), a_log, pad]),
                           jnp.concatenate([jnp.zeros((GDN_HEADS,), F32), dt_bias, pad])])

    tm = _pick(n, (1024, 512, 256))
    main, small = _in_proj(x2d, ln_w.reshape(1, d), w_main, w_small, tm, 2048)

    c = min(GDN_CHUNK, t)
    tb = _pick(t, (256, 128, 64))
    o_g, new_ssm = _gdn(main, small, conv0, ssm0, w_conv, gate_rows, gn_w.reshape(1, GDN_DV), batch, t, tb, c, 8)
    qkv_rows = main.reshape(batch, t, -1)[:, max(t - (CONV_W - 1), 0):, :3 * GDN_HEADS * GDN_DK].astype(F32)
    new_conv = jnp.concatenate([conv0, qkv_rows], axis=1)[:, -(CONV_W - 1):]

    p_len = past_lat.shape[1]
    cos, sin = _rope_tables(p_len, t)
    cos2 = jnp.concatenate([cos, cos], axis=1)
    sin2 = jnp.concatenate([-sin, sin], axis=1)
    qn_w2 = qn_w.reshape(1, Q_LORA)
    kvn_w2 = kvn_w.reshape(1, KV_LORA)
    if p_len == 0:
        wq_t = jnp.transpose(w_q_up, (1, 2, 0)).reshape(hh * QK_DIM, Q_LORA).astype(BF16)
        w_uk2 = w_uk.reshape(KV_LORA, hh * NOPE_DIM).astype(BF16)
        wuv_t = jnp.transpose(w_uv, (1, 2, 0)).reshape(hh * V_DIM, KV_LORA).astype(BF16)
        tmq = _pick(t, (1024, 512, 256, 128))
        q_t, k_all, v_t, lat, kr = _mla_prep(main, small, cos.T, sin.T, cos2, sin2, qn_w2, kvn_w2,
                                             wq_t, w_uk2, wuv_t, batch, t, tmq)
        tq = _pick(t, (2048, 1024, 512))
        o_m = _flash(q_t, k_all, v_t, main, batch, t, tq, 256)
    else:
        wq_n = jnp.transpose(w_q_up[:, :, :NOPE_DIM], (1, 0, 2)).astype(BF16)
        wq_r = jnp.transpose(w_q_up[:, :, NOPE_DIM:], (1, 0, 2))
        wq_rr = jnp.concatenate([wq_r[..., half:], wq_r[..., :half]], axis=-1).astype(BF16)
        wuk_t = jnp.transpose(w_uk, (1, 2, 0)).astype(BF16)
        wuv = jnp.transpose(w_uv, (1, 0, 2)).astype(BF16)
        o_m, lat, kr = _mla_decode(main, small, past_lat, past_kr, cos2, sin2, qn_w2, kvn_w2,
                                   wq_n, wq_r.astype(BF16), wq_rr, wuk_t, wuv, batch, t)

    w_g = w_out[:GDN_HEADS * GDN_DV].astype(BF16)
    w_m = w_out[GDN_HEADS * GDN_DV:].astype(BF16)
    y = _out_proj(x2d, o_g, o_m, w_g, w_m, lnf_w.reshape(1, d), _pick(n, (512, 256)))
    return y.reshape(batch, t, d), new_conv, new_ssm, lat, kr


def kernel(x_prompt, x_sample, cache_mla_latent, cache_mla_krope, state_gdn_conv, state_gdn_ssm, ln_in_w, w_in, w_conv, a_log, dt_bias, gdn_norm_w, q_norm_w, w_q_up, kv_norm_w, w_uk, w_uv, w_out, ln_final_w):
    depth = w_in.shape[0]
    assert depth == 1, "the final norm is fused into the out-projection of the single layer"
    bp = x_prompt.shape[0]
    lw = (ln_in_w[0], w_in[0], w_conv[0], a_log[0], dt_bias[0], gdn_norm_w[0],
          q_norm_w[0], w_q_up[0], kv_norm_w[0], w_uk[0], w_uv[0], w_out[0])
    zc = jnp.zeros((bp, CONV_W - 1, 3 * GDN_HEADS * GDN_DK), F32)
    zs = jnp.zeros((bp, GDN_HEADS, GDN_DK, GDN_DV), F32)
    yp, c1, s1, l1, r1 = _layer(x_prompt, zc, zs, jnp.zeros((bp, 0, KV_LORA), F32),
                                jnp.zeros((bp, 0, ROPE_DIM), F32), lw, ln_final_w)
    ys, c2, s2, l2, r2 = _layer(x_sample, state_gdn_conv[0], state_gdn_ssm[0], cache_mla_latent[0],
                                cache_mla_krope[0], lw, ln_final_w)
    return (yp, ys, c1[None], s1[None], l1[None], r1[None], c2[None], s2[None], l2[None], r2[None])
```
